```python
import math
import jax
import jax.numpy as jnp
from jax import lax
import numpy as np

D_MODEL = 1024
BATCH = 4
SEQ = 8192
DEPTH = 1
DEC_BATCH = 32
DEC_SEQ = 8
PAST_LEN = 16384
PAGE_SIZE = 128

D_MIX = D_MODEL
HEAD_DIM = 64
D_SB = D_MIX // 2
D_MB = D_MIX - D_SB
H_SB = D_SB // HEAD_DIM
H_MB = D_MB // HEAD_DIM
Q_BLOCK = 128
MOBA_BLOCK = 256
MOBA_TOPK = 3
N_GROUPS = 4
EXPERTS_PER_GROUP = 8
N_EXPERTS = N_GROUPS * EXPERTS_PER_GROUP
TOPK_IN_GROUP = 2
D_EXPERT = D_MODEL // 2
MOE_BLOCK = 128
RMS_EPS = 1e-6
SCALE = HEAD_DIM ** -0.5

kernel_name = 'hymba_stickbreak_moba_hiermoe_step'


def rmsnorm(x, g):
    xf = x.astype(jnp.float32)
    y = xf * lax.rsqrt(jnp.mean(xf * xf, axis=-1, keepdims=True) + RMS_EPS)
    return (y * g.astype(jnp.float32)).astype(x.dtype)


def alibi_slopes(n_heads):
    return 2.0 ** (-8.0 * jnp.arange(1, n_heads + 1, dtype=jnp.float32) / n_heads)


def mixer_inputs(x, norm_g, w_in, q_norm, k_norm):
    xn = rmsnorm(x, norm_g)
    proj = jnp.einsum('nld,de->nle', xn, w_in)
    cuts = [D_SB, 2 * D_SB, 3 * D_SB, 3 * D_SB + D_MB, 3 * D_SB + 2 * D_MB]
    q_sb, k_sb, v_sb, q_mb, k_mb, v_mb = jnp.split(proj, cuts, axis=-1)

    def heads(t, n_heads):
        return t.reshape(t.shape[0], t.shape[1], n_heads, HEAD_DIM)

    q_mb = rmsnorm(heads(q_mb, H_MB), q_norm)
    k_mb = rmsnorm(heads(k_mb, H_MB), k_norm)
    return (heads(q_sb, H_SB), heads(k_sb, H_SB), heads(v_sb, H_SB),
            q_mb, k_mb, heads(v_mb, H_MB))


def stick_breaking_block(q, t0, k, v):
    z = jnp.einsum('hqd,hsd->hqs', q, k).astype(jnp.float32) * SCALE
    t = t0 + jnp.arange(q.shape[1])
    s = jnp.arange(k.shape[1])
    causal = s[None, :] < t[:, None]
    log_keep = jnp.where(causal, jax.nn.log_sigmoid(-z), 0.0)
    log_after = lax.cumsum(log_keep, axis=2, reverse=True) - log_keep
    a = jnp.where(causal, jnp.exp(jax.nn.log_sigmoid(z) + log_after), 0.0)
    return jnp.einsum('hqs,hsd->hqd', a.astype(v.dtype), v)


def to_blocks(k):
    s_len = k.shape[2]
    n_blk = -(-s_len // MOBA_BLOCK)
    k = jnp.pad(k, ((0, 0), (0, 0), (0, n_blk * MOBA_BLOCK - s_len), (0, 0)))
    return k.reshape(k.shape[0], k.shape[1], n_blk, MOBA_BLOCK, HEAD_DIM)


def moba_block(q, t0, kb, vb, kmean, slopes):
    n_heads, n_q, _ = q.shape
    n_blk = kb.shape[1]
    t = t0 + jnp.arange(n_q)
    own = t0 // MOBA_BLOCK
    gate = jnp.einsum('hqd,hnd->hqn', q.astype(jnp.float32), kmean)
    gate = jnp.where(jnp.arange(n_blk)[None, None, :] < own, gate, -jnp.inf)
    k_top = min(MOBA_TOPK, n_blk)
    _, sel = lax.top_k(gate, k_top)
    slot_ok = jnp.arange(k_top) < own
    hi = jnp.arange(n_heads)[:, None, None]
    k_sel = kb[hi, sel]
    v_sel = vb[hi, sel]
    pos_sel = sel[..., None] * MOBA_BLOCK + jnp.arange(MOBA_BLOCK)
    s_sel = jnp.einsum('hqd,hqrkd->hqrk', q, k_sel).astype(jnp.float32) * SCALE
    s_sel = s_sel - slopes[:, None, None, None] * (t[None, :, None, None] - pos_sel)
    s_sel = jnp.where(slot_ok[None, None, :, None], s_sel, -jnp.inf)
    k_own = lax.dynamic_index_in_dim(kb, own, axis=1, keepdims=False)
    v_own = lax.dynamic_index_in_dim(vb, own, axis=1, keepdims=False)
    pos_own = own * MOBA_BLOCK + jnp.arange(MOBA_BLOCK)
    s_own = jnp.einsum('hqd,hkd->hqk', q, k_own).astype(jnp.float32) * SCALE
    s_own = s_own - slopes[:, None, None] * (t[:, None] - pos_own[None, :])
    s_own = jnp.where(pos_own[None, :] <= t[:, None], s_own, -jnp.inf)
    scores = jnp.concatenate([s_sel.reshape(n_heads, n_q, k_top * MOBA_BLOCK), s_own], axis=-1)
    p = jax.nn.softmax(scores, axis=-1).astype(vb.dtype)
    p_sel = p[..., :k_top * MOBA_BLOCK].reshape(n_heads, n_q, k_top, MOBA_BLOCK)
    p_own = p[..., k_top * MOBA_BLOCK:]
    return (jnp.einsum('hqrk,hqrkd->hqd', p_sel, v_sel)
            + jnp.einsum('hqk,hkd->hqd', p_own, v_own))


def prompt_mixers(q_sb, k_sb, v_sb, q_mb, kb_mb, vb_mb, km_mb, slopes):
    n, _, s_len, _ = q_sb.shape
    n_qb = s_len // Q_BLOCK

    def chunk(idx):
        i = idx // n_qb
        t0 = (idx % n_qb) * Q_BLOCK
        o_sb = stick_breaking_block(lax.dynamic_slice_in_dim(q_sb[i], t0, Q_BLOCK, axis=1),
                                    t0, k_sb[i], v_sb[i])
        o_mb = moba_block(lax.dynamic_slice_in_dim(q_mb[i], t0, Q_BLOCK, axis=1),
                          t0, kb_mb[i], vb_mb[i], km_mb[i], slopes)
        return o_sb, o_mb

    o_sb, o_mb = lax.map(chunk, jnp.arange(n * n_qb))

    def merge(o):
        hh = o.shape[1]
        o = o.reshape(n, n_qb, hh, Q_BLOCK, HEAD_DIM).transpose(0, 1, 3, 2, 4)
        return o.reshape(n, s_len, hh * HEAD_DIM)

    return merge(o_sb), merge(o_mb)


def sample_mixers(q_sb, k_sb, v_sb, q_mb, k_mb, v_mb,
                  cache_k_sb, cache_v_sb, cache_k_mb, cache_v_mb, page_table, slopes):
    past_len = page_table.shape[1] * PAGE_SIZE

    def full_seq(cache, new):
        past = cache[page_table].reshape(new.shape[0], past_len, new.shape[2], HEAD_DIM)
        return jnp.concatenate([past, new], axis=1).transpose(0, 2, 1, 3)

    def hm(t):
        return t.transpose(0, 2, 1, 3)

    o_sb = jax.vmap(lambda q, kk, vv: stick_breaking_block(q, past_len, kk, vv))(
        hm(q_sb), full_seq(cache_k_sb, k_sb), full_seq(cache_v_sb, v_sb))
    kb = to_blocks(full_seq(cache_k_mb, k_mb))
    vb = to_blocks(full_seq(cache_v_mb, v_mb))
    km = jnp.mean(kb.astype(jnp.float32), axis=-2)
    o_mb = jax.vmap(lambda q, kk, vv, mm: moba_block(q, past_len, kk, vv, mm, slopes))(
        hm(q_mb), kb, vb, km)

    def merge(o):
        return o.transpose(0, 2, 1, 3).reshape(o.shape[0], o.shape[2], o.shape[1] * HEAD_DIM)

    return merge(o_sb), merge(o_mb)


def mix_output(o_sb, o_mb, out_norm_sb, out_norm_mb, w_out):
    o = jnp.concatenate([rmsnorm(o_sb, out_norm_sb), rmsnorm(o_mb, out_norm_mb)], axis=-1)
    return jnp.einsum('nle,ed->nld', o, w_out)


def hier_moe(x, w_rg, b_rg, w_re, b_re, w_gate, w_up, w_down):
    n_tok, d = x.shape
    xf = x.astype(jnp.float32)
    p_grp = jax.nn.softmax(xf @ w_rg.astype(jnp.float32) + b_rg.astype(jnp.float32), axis=-1)
    grp = jnp.argmax(p_grp, axis=-1)
    p_top = jnp.take_along_axis(p_grp, grp[:, None], axis=-1)
    logit_e = (xf @ w_re.astype(jnp.float32) + b_re.astype(jnp.float32)).reshape(
        n_tok, N_GROUPS, EXPERTS_PER_GROUP)
    logit_in = jnp.take_along_axis(logit_e, grp[:, None, None], axis=1)[:, 0]
    top_logit, top_local = lax.top_k(logit_in, TOPK_IN_GROUP)
    gate = p_top * jax.nn.softmax(top_logit, axis=-1)
    eid = grp[:, None] * EXPERTS_PER_GROUP + top_local
    n_assign = n_tok * TOPK_IN_GROUP
    flat_e = eid.reshape(-1)
    flat_tok = jnp.repeat(jnp.arange(n_tok, dtype=jnp.int32), TOPK_IN_GROUP)
    order = jnp.argsort(flat_e)
    e_sorted = flat_e[order]
    tok_sorted = flat_tok[order]
    gate_sorted = gate.reshape(-1)[order]
    counts = jnp.zeros((N_EXPERTS,), jnp.int32).at[flat_e].add(1)
    padded = (counts + MOE_BLOCK - 1) // MOE_BLOCK * MOE_BLOCK
    pad_end = jnp.cumsum(padded)
    pad_start = pad_end - padded
    cnt_start = jnp.cumsum(counts) - counts
    dest = pad_start[e_sorted] + jnp.arange(n_assign) - cnt_start[e_sorted]
    n_blocks = -(-n_assign // MOE_BLOCK) + N_EXPERTS
    slot_tok = jnp.full((n_blocks * MOE_BLOCK,), n_tok, jnp.int32).at[dest].set(tok_sorted)
    block_expert = jnp.minimum(
        jnp.searchsorted(pad_end, jnp.arange(n_blocks) * MOE_BLOCK, side='right'), N_EXPERTS - 1)
    x_pad = jnp.concatenate([x, jnp.zeros((1, d), x.dtype)], axis=0)

    def expert_block(args):
        toks, e = args
        xb = x_pad[toks]
        hb = jax.nn.silu(xb @ w_gate[e]) * (xb @ w_up[e])
        return hb @ w_down[e]

    y_slots = lax.map(expert_block, (slot_tok.reshape(n_blocks, MOE_BLOCK), block_expert))
    y_assign = y_slots.reshape(-1, d)[dest].astype(jnp.float32)
    out = jnp.zeros((n_tok, d), jnp.float32).at[tok_sorted].add(y_assign * gate_sorted[:, None])
    return out.astype(x.dtype)


def setup_inputs(seed: int = 0) -> dict:
    key = jax.random.key(seed)
    ks = jax.random.split(key, 24)
    f32 = jnp.float32
    n_pages = PAST_LEN // PAGE_SIZE
    n_used = DEC_BATCH * n_pages
    n_pool = n_used + max(1, n_used // 4)

    def normal(k, shape, scale=1.0):
        return scale * jax.random.normal(k, shape, f32)

    def gain(k, shape):
        return 1.0 + 0.1 * jax.random.normal(k, shape, f32)

    page_table = jax.random.permutation(ks[6], n_pool)[:n_used].reshape(
        DEC_BATCH, n_pages).astype(jnp.int32)
    return {
        'x_prompt': normal(ks[0], (BATCH, SEQ, D_MODEL)),
        'x_sample': normal(ks[1], (DEC_BATCH, DEC_SEQ, D_MODEL)),
        'cache_k_sb': normal(ks[2], (DEPTH, n_pool, PAGE_SIZE, H_SB, HEAD_DIM)),
        'cache_v_sb': normal(ks[3], (DEPTH, n_pool, PAGE_SIZE, H_SB, HEAD_DIM)),
        'cache_k_mb': normal(ks[4], (DEPTH, n_pool, PAGE_SIZE, H_MB, HEAD_DIM)),
        'cache_v_mb': normal(ks[5], (DEPTH, n_pool, PAGE_SIZE, H_MB, HEAD_DIM)),
        'page_table': page_table,
        'norm_attn': gain(ks[7], (DEPTH, D_MODEL)),
        'w_in': normal(ks[8], (DEPTH, D_MODEL, 3 * D_MIX), D_MODEL ** -0.5),
        'q_norm_mb': gain(ks[9], (DEPTH, HEAD_DIM)),
        'k_norm_mb': gain(ks[10], (DEPTH, HEAD_DIM)),
        'out_norm_sb': gain(ks[11], (DEPTH, D_SB)),
        'out_norm_mb': gain(ks[12], (DEPTH, D_MB)),
        'w_out': normal(ks[13], (DEPTH, D_MIX, D_MODEL), D_MIX ** -0.5),
        'norm_ffn': gain(ks[14], (DEPTH, D_MODEL)),
        'w_router_group': normal(ks[15], (DEPTH, D_MODEL, N_GROUPS), D_MODEL ** -0.5),
        'b_router_group': normal(ks[16], (DEPTH, N_GROUPS), 0.01),
        'w_router_expert': normal(ks[17], (DEPTH, D_MODEL, N_EXPERTS), D_MODEL ** -0.5),
        'b_router_expert': normal(ks[18], (DEPTH, N_EXPERTS), 0.01),
        'w_gate': normal(ks[19], (DEPTH, N_EXPERTS, D_MODEL, D_EXPERT), D_MODEL ** -0.5),
        'w_up': normal(ks[20], (DEPTH, N_EXPERTS, D_MODEL, D_EXPERT), D_MODEL ** -0.5),
        'w_down': normal(ks[21], (DEPTH, N_EXPERTS, D_EXPERT, D_MODEL), D_EXPERT ** -0.5),
    }


def reference(x_prompt, x_sample, cache_k_sb, cache_v_sb, cache_k_mb, cache_v_mb, page_table,
              norm_attn, w_in, q_norm_mb, k_norm_mb, out_norm_sb, out_norm_mb, w_out,
              norm_ffn, w_router_group, b_router_group, w_router_expert, b_router_expert,
              w_gate, w_up, w_down):
    slopes = alibi_slopes(H_MB)
    n_p, s_p, d = x_prompt.shape
    n_s, s_s, _ = x_sample.shape
    xp, xs = x_prompt, x_sample
    nk_sb_p, nv_sb_p, nk_mb_p, nv_mb_p = [], [], [], []
    nk_sb_s, nv_sb_s, nk_mb_s, nv_mb_s = [], [], [], []
    for l in range(DEPTH):
        q1, k1, v1, q2, k2, v2 = mixer_inputs(xp, norm_attn[l], w_in[l], q_norm_mb[l], k_norm_mb[l])
        nk_sb_p.append(k1)
        nv_sb_p.append(v1)
        nk_mb_p.append(k2)
        nv_mb_p.append(v2)
        kb2 = to_blocks(k2.transpose(0, 2, 1, 3))
        vb2 = to_blocks(v2.transpose(0, 2, 1, 3))
        km2 = jnp.mean(kb2.astype(jnp.float32), axis=-2)
        o_sb_p, o_mb_p = prompt_mixers(q1.transpose(0, 2, 1, 3), k1.transpose(0, 2, 1, 3),
                                       v1.transpose(0, 2, 1, 3), q2.transpose(0, 2, 1, 3),
                                       kb2, vb2, km2, slopes)
        hp = xp + mix_output(o_sb_p, o_mb_p, out_norm_sb[l], out_norm_mb[l], w_out[l])
        q1, k1, v1, q2, k2, v2 = mixer_inputs(xs, norm_attn[l], w_in[l], q_norm_mb[l], k_norm_mb[l])
        nk_sb_s.append(k1)
        nv_sb_s.append(v1)
        nk_mb_s.append(k2)
        nv_mb_s.append(v2)
        o_sb_s, o_mb_s = sample_mixers(q1, k1, v1, q2, k2, v2, cache_k_sb[l], cache_v_sb[l],
                                       cache_k_mb[l], cache_v_mb[l], page_table, slopes)
        hs = xs + mix_output(o_sb_s, o_mb_s, out_norm_sb[l], out_norm_mb[l], w_out[l])
        tokens = jnp.concatenate([hp.reshape(-1, d), hs.reshape(-1, d)], axis=0)
        tokens = tokens + hier_moe(rmsnorm(tokens, norm_ffn[l]), w_router_group[l], b_router_group[l],
                                   w_router_expert[l], b_router_expert[l],
                                   w_gate[l], w_up[l], w_down[l])
        xp = tokens[:n_p * s_p].reshape(n_p, s_p, d)
        xs = tokens[n_p * s_p:].reshape(n_s, s_s, d)
    return (xp, xs,
            jnp.stack(nk_sb_p), jnp.stack(nv_sb_p), jnp.stack(nk_mb_p), jnp.stack(nv_mb_p),
            jnp.stack(nk_sb_s), jnp.stack(nv_sb_s), jnp.stack(nk_mb_s), jnp.stack(nv_mb_s))
```

```python
import functools

import jax
import jax.numpy as jnp
from jax import lax
from jax.experimental import pallas as pl
from jax.experimental.pallas import tpu as pltpu

F32 = jnp.float32
BF16 = jnp.bfloat16
I32 = jnp.int32

HEAD_DIM = 64
PAIR = 128
MOBA_BLOCK = 256
MOBA_TOPK = 3
N_GROUPS = 4
EXPERTS_PER_GROUP = 8
N_EXPERTS = N_GROUPS * EXPERTS_PER_GROUP
RMS_EPS = 1e-6
SCALE = HEAD_DIM ** -0.5
QB = 128
KB = 128
TOK_TILE = 256
SLOT_BLOCK = 128
ROUTE_LANE0 = N_GROUPS
SB_SKIP = 110.0
NEG = -1e30
VMEM_LIMIT = 56 * 1024 * 1024


def _cparams(sem):
    return pltpu.CompilerParams(dimension_semantics=sem, vmem_limit_bytes=VMEM_LIMIT)


def _split_bf16(x):
    hi = x.astype(BF16)
    lo = (x - hi.astype(F32)).astype(BF16)
    return hi, lo


def _dot(a, b):
    return jnp.dot(a, b, preferred_element_type=F32)


def _dot_nt(a, b):
    return lax.dot_general(a, b, (((1,), (1,)), ((), ())), preferred_element_type=F32)


def _proj_kernel(x_ref, g_ref, w_ref, qn_ref, kn_ref, seg_ref,
                 q1_ref, k1_ref, v1_ref, q2_ref, k2_ref, v2_ref,
                 q1b_ref, k1b_ref, v1b_ref, q2b_ref, k2b_ref, v2b_ref, km_ref):
    x = x_ref[...]
    ms = jnp.mean(x * x, axis=-1, keepdims=True)
    xn = x * lax.rsqrt(ms + RMS_EPS) * g_ref[...]
    proj = _dot(xn.astype(BF16), w_ref[...])
    dh = q1_ref.shape[1]
    seg = seg_ref[...]

    def headnorm(t, g):
        hi, lo = _split_bf16(t * t)
        ssum = _dot(hi, seg) + _dot(lo, seg)
        return t * lax.rsqrt(ssum * (1.0 / HEAD_DIM) + RMS_EPS) * g

    q1 = proj[:, 0 * dh:1 * dh]
    k1 = proj[:, 1 * dh:2 * dh]
    v1 = proj[:, 2 * dh:3 * dh]
    q2 = headnorm(proj[:, 3 * dh:4 * dh], qn_ref[...])
    k2 = headnorm(proj[:, 4 * dh:5 * dh], kn_ref[...])
    v2 = proj[:, 5 * dh:6 * dh]
    for ref, bref, val in ((q1_ref, q1b_ref, q1), (k1_ref, k1b_ref, k1), (v1_ref, v1b_ref, v1),
                           (q2_ref, q2b_ref, q2), (k2_ref, k2b_ref, k2), (v2_ref, v2b_ref, v2)):
        ref[...] = val
        bref[...] = val.astype(BF16)
    for c in range(x.shape[0] // MOBA_BLOCK):
        blk = k2[c * MOBA_BLOCK:(c + 1) * MOBA_BLOCK]
        km_ref[c] = jnp.sum(blk, axis=0, keepdims=True) * (1.0 / MOBA_BLOCK)


def _project(x, norm_g, w_in_b, q_norm, k_norm):
    t, d = x.shape
    dh = w_in_b.shape[1] // 6
    n_heads = dh // HEAD_DIM
    assert t % TOK_TILE == 0 and TOK_TILE % MOBA_BLOCK == 0
    r = jnp.arange(dh) // HEAD_DIM
    seg = (r[:, None] == r[None, :]).astype(BF16)
    qn = jnp.tile(q_norm, n_heads)[None]
    kn = jnp.tile(k_norm, n_heads)[None]
    tile = lambda i: (i, 0)
    const = lambda i: (0, 0)
    f_spec = pl.BlockSpec((TOK_TILE, dh), tile)
    n_km = TOK_TILE // MOBA_BLOCK
    outs = pl.pallas_call(
        _proj_kernel,
        grid=(t // TOK_TILE,),
        in_specs=[pl.BlockSpec((TOK_TILE, d), tile), pl.BlockSpec((1, d), const),
                  pl.BlockSpec(w_in_b.shape, const), pl.BlockSpec((1, dh), const),
                  pl.BlockSpec((1, dh), const), pl.BlockSpec((dh, dh), const)],
        out_specs=[f_spec] * 12 + [pl.BlockSpec((n_km, 1, dh), lambda i: (i, 0, 0))],
        out_shape=[jax.ShapeDtypeStruct((t, dh), F32)] * 6
        + [jax.ShapeDtypeStruct((t, dh), BF16)] * 6
        + [jax.ShapeDtypeStruct((t // MOBA_BLOCK, 1, dh), F32)],
        compiler_params=_cparams(("parallel",)),
        name="in_proj",
    )(x, norm_g[None], w_in_b, qn, kn, seg)
    return outs


def _sb_block(qh, k, v, carry, acc, cum_u, valid):
    z = _dot_nt(qh, k) * SCALE
    lk = -(jnp.maximum(z, 0.0) + jnp.log1p(jnp.exp(-jnp.abs(z))))
    if valid is not None:
        lk = jnp.where(valid, lk, 0.0)
    hi, lo = _split_bf16(lk)
    cs = _dot(hi, cum_u) + _dot(lo, cum_u)
    a = jnp.exp(z + cs + carry)
    if valid is not None:
        a = jnp.where(valid, a, 0.0)
    acc = acc + _dot(a.astype(BF16), v)
    carry = carry + cs[:, 0:1]
    return carry, acc


def _cum_matrix(n):
    r = lax.broadcasted_iota(I32, (n, n), 0)
    c = lax.broadcasted_iota(I32, (n, n), 1)
    return (r >= c).astype(BF16)


def _sb_prompt_kernel(q_ref, k_ref, v_ref, o_ref):
    qb = pl.program_id(2)
    q = q_ref[...]
    lane = lax.broadcasted_iota(I32, (1, PAIR), 1)
    head0 = lane < HEAD_DIM
    zero = jnp.zeros_like(q)
    qh = (jnp.where(head0, q, zero), jnp.where(head0, zero, q))
    cum_u = _cum_matrix(KB)
    r = lax.broadcasted_iota(I32, (QB, KB), 0)
    c = lax.broadcasted_iota(I32, (QB, KB), 1)
    strictly_before = c < r

    def load(kb):
        start = pl.multiple_of(kb * KB, KB)
        return k_ref[pl.ds(start, KB), :], v_ref[pl.ds(start, KB), :]

    k, v = load(qb)
    carries, accs = [], []
    for h in range(2):
        cr, ac = _sb_block(qh[h], k, v, jnp.zeros((QB, 1), F32), jnp.zeros((QB, PAIR), F32),
                           cum_u, strictly_before)
        carries.append(cr)
        accs.append(ac)

    def top(c0, c1):
        return jnp.max(jnp.maximum(c0, c1))

    def cond(st):
        kb, cmax = st[0], st[1]
        return jnp.logical_and(kb >= 0, cmax > -SB_SKIP)

    def body(st):
        kb, _, c0, c1, a0, a1 = st
        k, v = load(kb)
        c0, a0 = _sb_block(qh[0], k, v, c0, a0, cum_u, None)
        c1, a1 = _sb_block(qh[1], k, v, c1, a1, cum_u, None)
        return kb - 1, top(c0, c1), c0, c1, a0, a1

    st = lax.while_loop(cond, body, (qb - 1, top(carries[0], carries[1]),
                                     carries[0], carries[1], accs[0], accs[1]))
    o_ref[...] = jnp.where(head0, st[4], st[5])


def _sb_prompt(q1b, k1b, v1b, n_seq, seq_len):
    t_all, dh = q1b.shape
    n_pairs = dh // PAIR
    n_qb = seq_len // QB
    return pl.pallas_call(
        _sb_prompt_kernel,
        grid=(n_seq, n_pairs, n_qb),
        in_specs=[pl.BlockSpec((QB, PAIR), lambda n, j, i: (n * n_qb + i, j)),
                  pl.BlockSpec((seq_len, PAIR), lambda n, j, i: (n, j)),
                  pl.BlockSpec((seq_len, PAIR), lambda n, j, i: (n, j))],
        out_specs=pl.BlockSpec((QB, PAIR), lambda n, j, i: (n * n_qb + i, j)),
        out_shape=jax.ShapeDtypeStruct((t_all, dh), F32),
        compiler_params=_cparams(("parallel", "parallel", "arbitrary")),
        name="sb_prompt",
    )(q1b, k1b, v1b)


def _top_blocks(gate, n_past, blk_iota):
    g = jnp.where(blk_iota < n_past, gate, NEG)
    big = jnp.int32(2 ** 30)
    sel = jnp.zeros(gate.shape, F32)
    for r in range(MOBA_TOPK):
        m = jnp.max(g, axis=-1, keepdims=True)
        idx = jnp.min(jnp.where(g == m, blk_iota, big), axis=-1, keepdims=True)
        pick = jnp.logical_and(blk_iota == idx, r < n_past)
        sel = jnp.where(pick, 1.0, sel)
        g = jnp.where(pick, 2.0 * NEG, g)
    return sel


def _softmax_step(s, ok, v, m, l, acc):
    s = jnp.where(ok, s, NEG)
    m_new = jnp.maximum(m, jnp.max(s, axis=-1, keepdims=True))
    alpha = jnp.exp(m - m_new)
    p = jnp.where(ok, jnp.exp(s - m_new), 0.0)
    l = alpha * l + jnp.sum(p, axis=-1, keepdims=True)
    acc = alpha * acc + _dot(p.astype(BF16), v)
    return m_new, l, acc


def _moba_prompt_kernel(slopes_ref, q_ref, k_ref, v_ref, km_ref, o_ref):
    j = pl.program_id(1)
    qb = pl.program_id(2)
    q = q_ref[...]
    lane = lax.broadcasted_iota(I32, (1, PAIR), 1)
    head0 = lane < HEAD_DIM
    zero = jnp.zeros_like(q)
    qh = (jnp.where(head0, q, zero), jnp.where(head0, zero, q))
    own = (qb * QB) // MOBA_BLOCK
    q_off = qb * QB - own * MOBA_BLOCK
    nbp = km_ref.shape[0]
    blk_iota = lax.broadcasted_iota(I32, (QB, nbp), 1)
    km_hi, km_lo = _split_bf16(km_ref[...])
    r = lax.broadcasted_iota(I32, (QB, MOBA_BLOCK), 0)
    c = lax.broadcasted_iota(I32, (QB, MOBA_BLOCK), 1)
    rel = (r - c).astype(F32)
    sels, slopes = [], []
    for h in range(2):
        gate = _dot_nt(qh[h], km_hi) + _dot_nt(qh[h], km_lo)
        sels.append(_top_blocks(gate, own, blk_iota))
        slopes.append(slopes_ref[2 * j + h])

    def load(n):
        start = pl.multiple_of(n * MOBA_BLOCK, MOBA_BLOCK)
        return k_ref[pl.ds(start, MOBA_BLOCK), :], v_ref[pl.ds(start, MOBA_BLOCK), :]

    def scores(h, k, dist0):
        return _dot_nt(qh[h], k) * SCALE - slopes[h] * (rel + dist0)

    def body(n, st):
        k, v = load(n)
        dist0 = (qb * QB - n * MOBA_BLOCK).astype(F32)
        out = []
        for h in range(2):
            m, l, acc = st[3 * h:3 * h + 3]
            picked = jnp.max(jnp.where(blk_iota == n, sels[h], 0.0), axis=-1, keepdims=True)
            ok = jnp.broadcast_to(picked, (QB, MOBA_BLOCK)) > 0.0
            out.extend(_softmax_step(scores(h, k, dist0), ok, v, m, l, acc))
        return tuple(out)

    init = (jnp.full((QB, 1), NEG, F32), jnp.zeros((QB, 1), F32), jnp.zeros((QB, PAIR), F32)) * 2
    st = lax.fori_loop(0, own, body, init)
    k, v = load(own)
    causal = c <= r + q_off
    res = []
    for h in range(2):
        m, l, acc = st[3 * h:3 * h + 3]
        m, l, acc = _softmax_step(scores(h, k, q_off.astype(F32)), causal, v, m, l, acc)
        res.append(acc / l)
    o_ref[...] = jnp.where(head0, res[0], res[1])


def _moba_prompt(slopes, q2b, k2b, v2b, km, n_seq, seq_len):
    t_all, dh = q2b.shape
    n_pairs = dh // PAIR
    n_qb = seq_len // QB
    nbp = km.shape[0] // n_seq
    return pl.pallas_call(
        _moba_prompt_kernel,
        grid=(n_seq, n_pairs, n_qb),
        in_specs=[pl.BlockSpec(memory_space=pltpu.SMEM),
                  pl.BlockSpec((QB, PAIR), lambda n, j, i: (n * n_qb + i, j)),
                  pl.BlockSpec((seq_len, PAIR), lambda n, j, i: (n, j)),
                  pl.BlockSpec((seq_len, PAIR), lambda n, j, i: (n, j)),
                  pl.BlockSpec((nbp, PAIR), lambda n, j, i: (n, j))],
        out_specs=pl.BlockSpec((QB, PAIR), lambda n, j, i: (n * n_qb + i, j)),
        out_shape=jax.ShapeDtypeStruct((t_all, dh), F32),
        compiler_params=_cparams(("parallel", "parallel", "arbitrary")),
        name="moba_prompt",
    )(slopes, q2b, k2b, v2b, km)


def _head_rows(q, n_heads):
    s, dh = q.shape
    rows = jnp.concatenate([q] * n_heads, axis=0)
    rr = lax.broadcasted_iota(I32, (n_heads * s, dh), 0) // s
    ll = lax.broadcasted_iota(I32, (n_heads * s, dh), 1) // HEAD_DIM
    keep = rr == ll
    return jnp.where(keep, rows, 0.0), keep


def _merge_head_rows(acc, keep, n_heads):
    s = acc.shape[0] // n_heads
    masked = jnp.where(keep, acc, 0.0)
    out = masked[0:s]
    for h in range(1, n_heads):
        out = out + masked[h * s:(h + 1) * s]
    return out


def _sb_sample_kernel(pt_ref, q_ref, kn_ref, vn_ref, kp_ref, vp_ref, o_ref,
                      carry_ref, acc_ref, *, n_heads):
    del pt_ref
    p = pl.program_id(1)
    s = q_ref.shape[0]
    qrows, keep = _head_rows(q_ref[...], n_heads)
    qrows = qrows.astype(BF16)
    rows = n_heads * s
    cum_u = _cum_matrix(KB)

    @pl.when(p == 0)
    def _():
        r = lax.broadcasted_iota(I32, (rows, KB), 0) % s
        c = lax.broadcasted_iota(I32, (rows, KB), 1)
        carry, acc = _sb_block(qrows, kn_ref[...].astype(BF16), vn_ref[...].astype(BF16),
                               jnp.zeros((rows, 1), F32), jnp.zeros(acc_ref.shape, F32), cum_u, c < r)
        carry_ref[...] = jnp.broadcast_to(carry, carry_ref.shape)
        acc_ref[...] = acc

    @pl.when(jnp.logical_and(p > 0, jnp.max(carry_ref[...]) > -SB_SKIP))
    def _():
        carry, acc = _sb_block(qrows, kp_ref[0].astype(BF16), vp_ref[0].astype(BF16),
                               carry_ref[:, 0:1], acc_ref[...], cum_u, None)
        carry_ref[...] = jnp.broadcast_to(carry, carry_ref.shape)
        acc_ref[...] = acc

    @pl.when(p == pl.num_programs(1) - 1)
    def _():
        o_ref[...] = _merge_head_rows(acc_ref[...], keep, n_heads)


def _sb_sample(page_table, q1, k_new, v_new, cache_k, cache_v):
    n_b, n_pages = page_table.shape
    dh = q1.shape[1]
    s = q1.shape[0] // n_b
    page = cache_k.shape[1]
    assert page == KB
    n_heads = dh // HEAD_DIM
    rows = n_heads * s

    def page_idx(b, p, pt):
        return (pt[b, n_pages - jnp.maximum(p, 1)], 0, 0)

    grid_spec = pltpu.PrefetchScalarGridSpec(
        num_scalar_prefetch=1,
        grid=(n_b, n_pages + 1),
        in_specs=[pl.BlockSpec((s, dh), lambda b, p, pt: (b, 0)),
                  pl.BlockSpec((KB, dh), lambda b, p, pt: (b, 0)),
                  pl.BlockSpec((KB, dh), lambda b, p, pt: (b, 0)),
                  pl.BlockSpec((1, page, dh), page_idx),
                  pl.BlockSpec((1, page, dh), page_idx)],
        out_specs=pl.BlockSpec((s, dh), lambda b, p, pt: (b, 0)),
        scratch_shapes=[pltpu.VMEM((rows, PAIR), F32), pltpu.VMEM((rows, dh), F32)],
    )
    return pl.pallas_call(
        functools.partial(_sb_sample_kernel, n_heads=n_heads),
        grid_spec=grid_spec,
        out_shape=jax.ShapeDtypeStruct(q1.shape, F32),
        compiler_params=_cparams(("parallel", "arbitrary")),
        name="sb_sample",
    )(page_table, q1, k_new, v_new, cache_k, cache_v)


def _kmean_kernel(pt_ref, k0_ref, k1_ref, o_ref):
    del pt_ref
    tot = jnp.sum(k0_ref[0], axis=0, keepdims=True) + jnp.sum(k1_ref[0], axis=0, keepdims=True)
    o_ref[0] = tot * (1.0 / MOBA_BLOCK)


def _past_block_means(page_table, cache_k):
    n_b, n_pages = page_table.shape
    _, page, dh = cache_k.shape
    assert 2 * page == MOBA_BLOCK
    n_blk = n_pages // 2
    grid_spec = pltpu.PrefetchScalarGridSpec(
        num_scalar_prefetch=1,
        grid=(n_b, n_blk),
        in_specs=[pl.BlockSpec((1, page, dh), lambda b, m, pt: (pt[b, 2 * m], 0, 0)),
                  pl.BlockSpec((1, page, dh), lambda b, m, pt: (pt[b, 2 * m + 1], 0, 0))],
        out_specs=pl.BlockSpec((1, 1, dh), lambda b, m, pt: (b * n_blk + m, 0, 0)),
    )
    return pl.pallas_call(
        _kmean_kernel,
        grid_spec=grid_spec,
        out_shape=jax.ShapeDtypeStruct((n_b * n_blk, 1, dh), F32),
        compiler_params=_cparams(("parallel", "arbitrary")),
        name="past_block_means",
    )(page_table, cache_k, cache_k)


def _moba_sample_kernel(pt_ref, q_ref, slope_ref, km_ref, kn_ref, vn_ref,
                        k0_ref, k1_ref, v0_ref, v1_ref, o_ref,
                        sel_ref, m_ref, l_ref, acc_ref, *, n_heads, past_len):
    del pt_ref
    blk = pl.program_id(1)
    n_blk = pl.num_programs(1) - 1
    s = q_ref.shape[0]
    rows = n_heads * s
    qrows, keep = _head_rows(q_ref[...], n_heads)
    qb16 = qrows.astype(BF16)
    nbp = sel_ref.shape[1]
    blk_iota = lax.broadcasted_iota(I32, (rows, nbp), 1)
    slope = slope_ref[:, 0:1]
    r = lax.broadcasted_iota(I32, (rows, MOBA_BLOCK), 0) % s
    c = lax.broadcasted_iota(I32, (rows, MOBA_BLOCK), 1)

    @pl.when(blk == 0)
    def _():
        km_hi, km_lo = _split_bf16(km_ref[0])
        gate = _dot_nt(qb16, km_hi) + _dot_nt(qb16, km_lo)
        sel_ref[...] = _top_blocks(gate, n_blk, blk_iota)
        m_ref[...] = jnp.full(m_ref.shape, NEG, F32)
        l_ref[...] = jnp.zeros(l_ref.shape, F32)
        acc_ref[...] = jnp.zeros(acc_ref.shape, F32)

    def update(k, v, ok, dist0):
        sc = _dot_nt(qb16, k) * SCALE - slope * ((r - c).astype(F32) + dist0)
        m, l, acc = _softmax_step(sc, ok, v, m_ref[:, 0:1], l_ref[:, 0:1], acc_ref[...])
        m_ref[...] = jnp.broadcast_to(m, m_ref.shape)
        l_ref[...] = jnp.broadcast_to(l, l_ref.shape)
        acc_ref[...] = acc

    @pl.when(blk < n_blk)
    def _():
        k = jnp.concatenate([k0_ref[0], k1_ref[0]], axis=0).astype(BF16)
        v = jnp.concatenate([v0_ref[0], v1_ref[0]], axis=0).astype(BF16)
        picked = jnp.max(jnp.where(blk_iota == blk, sel_ref[...], 0.0), axis=-1, keepdims=True)
        dist0 = (past_len - blk * MOBA_BLOCK).astype(F32)
        update(k, v, jnp.broadcast_to(picked, (rows, MOBA_BLOCK)) > 0.0, dist0)

    @pl.when(blk == n_blk)
    def _():
        update(kn_ref[...].astype(BF16), vn_ref[...].astype(BF16), c <= r, jnp.float32(0.0))
        o_ref[...] = _merge_head_rows(acc_ref[...] / l_ref[:, 0:1], keep, n_heads)


def _moba_sample(page_table, q2, slope_rows, km, k_new, v_new, cache_k, cache_v):
    n_b, n_pages = page_table.shape
    dh = q2.shape[1]
    s = q2.shape[0] // n_b
    page = cache_k.shape[1]
    n_blk = n_pages // 2
    n_heads = dh // HEAD_DIM
    rows = n_heads * s
    nbp = km.shape[1]
    last = n_blk - 1

    def pg(off):
        return lambda b, m, pt: (pt[b, 2 * jnp.minimum(m, last) + off], 0, 0)

    per_b = lambda b, m, pt: (b, 0)
    grid_spec = pltpu.PrefetchScalarGridSpec(
        num_scalar_prefetch=1,
        grid=(n_b, n_blk + 1),
        in_specs=[pl.BlockSpec((s, dh), per_b),
                  pl.BlockSpec((rows, PAIR), lambda b, m, pt: (0, 0)),
                  pl.BlockSpec((1, nbp, dh), lambda b, m, pt: (b, 0, 0)),
                  pl.BlockSpec((MOBA_BLOCK, dh), per_b),
                  pl.BlockSpec((MOBA_BLOCK, dh), per_b),
                  pl.BlockSpec((1, page, dh), pg(0)), pl.BlockSpec((1, page, dh), pg(1)),
                  pl.BlockSpec((1, page, dh), pg(0)), pl.BlockSpec((1, page, dh), pg(1))],
        out_specs=pl.BlockSpec((s, dh), lambda b, m, pt: (b, 0)),
        scratch_shapes=[pltpu.VMEM((rows, nbp), F32), pltpu.VMEM((rows, PAIR), F32),
                        pltpu.VMEM((rows, PAIR), F32), pltpu.VMEM((rows, dh), F32)],
    )
    return pl.pallas_call(
        functools.partial(_moba_sample_kernel, n_heads=n_heads, past_len=n_pages * page),
        grid_spec=grid_spec,
        out_shape=jax.ShapeDtypeStruct(q2.shape, F32),
        compiler_params=_cparams(("parallel", "arbitrary")),
        name="moba_sample",
    )(page_table, q2, slope_rows, km, k_new, v_new, cache_k, cache_k, cache_v, cache_v)


def _mix_route_kernel(cnt_in_ref, o1_ref, o2_ref, x_ref, g1_ref, g2_ref, wo_ref, gf_ref,
                      wr_ref, br_ref, h_ref, xn_ref, slab_ref, cnt_ref, run_ref):
    i = pl.program_id(0)

    @pl.when(i == 0)
    def _():
        run_ref[...] = cnt_in_ref[...]

    def rms(t, g):
        return t * lax.rsqrt(jnp.mean(t * t, axis=-1, keepdims=True) + RMS_EPS) * g

    n1 = rms(o1_ref[...], g1_ref[...]).astype(BF16)
    n2 = rms(o2_ref[...], g2_ref[...]).astype(BF16)
    dh = n1.shape[1]
    h = x_ref[...] + _dot(n1, wo_ref[0:dh, :]) + _dot(n2, wo_ref[dh:2 * dh, :])
    h_ref[...] = h
    xn = rms(h, gf_ref[...])
    xn_ref[...] = xn

    x_hi, x_lo = _split_bf16(xn)
    w_hi, w_lo = _split_bf16(wr_ref[...])
    logits = _dot(x_hi, w_hi) + _dot(x_hi, w_lo) + _dot(x_lo, w_hi) + br_ref[...]
    tl, nl = logits.shape
    lane = lax.broadcasted_iota(I32, (tl, nl), 1)
    big = jnp.int32(2 ** 30)

    def first_max(vals):
        m = jnp.max(vals, axis=-1, keepdims=True)
        idx = jnp.min(jnp.where(vals == m, lane, big), axis=-1, keepdims=True)
        return m, idx

    grp_logit = jnp.where(lane < N_GROUPS, logits, NEG)
    gm, grp = first_max(grp_logit)
    denom = jnp.sum(jnp.where(lane < N_GROUPS, jnp.exp(grp_logit - gm), 0.0), axis=-1, keepdims=True)
    p_top = 1.0 / denom
    e_lo = ROUTE_LANE0 + grp * EXPERTS_PER_GROUP
    in_grp = jnp.logical_and(lane >= e_lo, lane < e_lo + EXPERTS_PER_GROUP)
    e_logit = jnp.where(in_grp, logits, NEG)
    m1, i1 = first_max(e_logit)
    m2, i2 = first_max(jnp.where(lane == i1, 2.0 * NEG, e_logit))
    e2 = jnp.exp(m2 - m1)
    gate1 = p_top / (1.0 + e2)
    gate2 = p_top * e2 / (1.0 + e2)

    hot1 = lane == i1
    hot2 = lane == i2
    both = jnp.logical_or(hot1, hot2)
    rr = lax.broadcasted_iota(I32, (tl, tl), 0)
    cc = lax.broadcasted_iota(I32, (tl, tl), 1)
    earlier = (cc < rr).astype(BF16)
    before = _dot(earlier, jnp.where(both, 1.0, 0.0).astype(BF16)) + run_ref[...]
    rank1 = jnp.sum(jnp.where(hot1, before, 0.0), axis=-1, keepdims=True)
    rank2 = jnp.sum(jnp.where(hot2, before, 0.0), axis=-1, keepdims=True)
    run_ref[...] = run_ref[...] + jnp.sum(jnp.where(both, 1.0, 0.0), axis=0, keepdims=True)
    cnt_ref[...] = run_ref[...]

    eid1 = (i1 - ROUTE_LANE0).astype(F32)
    eid2 = (i2 - ROUTE_LANE0).astype(F32)
    cols = (eid1, eid2, rank1, rank2, gate1, gate2)
    slab = jnp.zeros((tl, nl), F32)
    for n, col in enumerate(cols):
        slab = jnp.where(lane == n, col, slab)
    slab_ref[...] = slab


def _mix_route(cnt_in, o1, o2, x, g1, g2, w_out_b, g_ffn, w_route, b_route):
    t, d = x.shape
    dh = o1.shape[1]
    assert t % TOK_TILE == 0
    tile = lambda i: (i, 0)
    const = lambda i: (0, 0)
    return pl.pallas_call(
        _mix_route_kernel,
        grid=(t // TOK_TILE,),
        in_specs=[pl.BlockSpec((1, PAIR), const),
                  pl.BlockSpec((TOK_TILE, dh), tile),
                  pl.BlockSpec((TOK_TILE, dh), tile),
                  pl.BlockSpec((TOK_TILE, d), tile),
                  pl.BlockSpec((1, dh), const), pl.BlockSpec((1, dh), const),
                  pl.BlockSpec((d, d), const), pl.BlockSpec((1, d), const),
                  pl.BlockSpec((d, PAIR), const), pl.BlockSpec((1, PAIR), const)],
        out_specs=[pl.BlockSpec((TOK_TILE, d), tile), pl.BlockSpec((TOK_TILE, d), tile),
                   pl.BlockSpec((TOK_TILE, PAIR), tile), pl.BlockSpec((1, PAIR), const)],
        out_shape=[jax.ShapeDtypeStruct((t, d), F32), jax.ShapeDtypeStruct((t, d), F32),
                   jax.ShapeDtypeStruct((t, PAIR), F32), jax.ShapeDtypeStruct((1, PAIR), F32)],
        scratch_shapes=[pltpu.VMEM((1, PAIR), F32)],
        compiler_params=_cparams(("arbitrary",)),
        name="mix_route",
    )(cnt_in, o1, o2, x, g1[None], g2[None], w_out_b, g_ffn[None], w_route, b_route)


def _row_copy(src, src_row, dst, dst_row, sem):
    return pltpu.make_async_copy(src.at[pl.ds(src_row, 1)], dst.at[pl.ds(dst_row, 1)], sem)


def _load_route(route_hbm, route_smem, sem):
    cp = pltpu.make_async_copy(route_hbm.at[pl.program_id(0)], route_smem, sem)
    cp.start()
    cp.wait()


def _dispatch_kernel(start_ref, route_hbm, xn_ref, xs_in, xs_out, route_smem, rsem, sem):
    del xs_in
    _load_route(route_hbm, route_smem, rsem)
    tl = xn_ref.shape[0]

    def slot(i, k):
        return start_ref[route_smem[4 * i + k]] + route_smem[4 * i + 2 + k]

    def issue(i, carry):
        _row_copy(xn_ref, i, xs_out, slot(i, 0), sem).start()
        _row_copy(xn_ref, i, xs_out, slot(i, 1), sem).start()
        return carry

    def drain(i, carry):
        _row_copy(xn_ref, 0, xs_out, 0, sem).wait()
        _row_copy(xn_ref, 0, xs_out, 0, sem).wait()
        return carry

    lax.fori_loop(0, tl, issue, 0)
    lax.fori_loop(0, tl, drain, 0)


def _dispatch(pad_start, route_i, xn, xs):
    t, d = xn.shape
    n_tiles = t // TOK_TILE
    grid_spec = pltpu.PrefetchScalarGridSpec(
        num_scalar_prefetch=1,
        grid=(n_tiles,),
        in_specs=[pl.BlockSpec(memory_space=pl.ANY),
                  pl.BlockSpec((TOK_TILE, d), lambda i, st: (i, 0)),
                  pl.BlockSpec(memory_space=pl.ANY)],
        out_specs=pl.BlockSpec(memory_space=pl.ANY),
        scratch_shapes=[pltpu.SMEM((4 * TOK_TILE,), I32), pltpu.SemaphoreType.DMA,
                        pltpu.SemaphoreType.DMA],
    )
    return pl.pallas_call(
        _dispatch_kernel,
        grid_spec=grid_spec,
        out_shape=jax.ShapeDtypeStruct(xs.shape, xs.dtype),
        input_output_aliases={3: 0},
        compiler_params=_cparams(("arbitrary",)),
        name="dispatch",
    )(pad_start, route_i, xn, xs)


def _expert_kernel(be_ref, x_ref, wg_ref, wu_ref, wd_ref, y_ref):
    del be_ref
    x = x_ref[...].astype(BF16)
    g = _dot(x, wg_ref[0])
    u = _dot(x, wu_ref[0])
    hid = g * (1.0 / (1.0 + jnp.exp(-g))) * u
    y_ref[...] = _dot(hid.astype(BF16), wd_ref[0])


def _experts(block_expert, xs, wg_b, wu_b, wd_b):
    n_slots, d = xs.shape
    de = wg_b.shape[2]
    n_blocks = n_slots // SLOT_BLOCK
    grid_spec = pltpu.PrefetchScalarGridSpec(
        num_scalar_prefetch=1,
        grid=(n_blocks,),
        in_specs=[pl.BlockSpec((SLOT_BLOCK, d), lambda b, be: (b, 0)),
                  pl.BlockSpec((1, d, de), lambda b, be: (be[b], 0, 0)),
                  pl.BlockSpec((1, d, de), lambda b, be: (be[b], 0, 0)),
                  pl.BlockSpec((1, de, d), lambda b, be: (be[b], 0, 0))],
        out_specs=pl.BlockSpec((SLOT_BLOCK, d), lambda b, be: (b, 0)),
    )
    return pl.pallas_call(
        _expert_kernel,
        grid_spec=grid_spec,
        out_shape=jax.ShapeDtypeStruct((n_slots, d), F32),
        compiler_params=_cparams(("arbitrary",)),
        name="experts",
    )(block_expert, xs, wg_b, wu_b, wd_b)


def _combine_kernel(start_ref, route_hbm, h_ref, slab_ref, ys_hbm, out_ref,
                    route_smem, y_buf, rsem, sem):
    _load_route(route_hbm, route_smem, rsem)
    tl = h_ref.shape[0]

    def slot(i, k):
        return start_ref[route_smem[4 * i + k]] + route_smem[4 * i + 2 + k]

    def issue(i, carry):
        _row_copy(ys_hbm, slot(i, 0), y_buf.at[0], i, sem).start()
        _row_copy(ys_hbm, slot(i, 1), y_buf.at[1], i, sem).start()
        return carry

    def drain(i, carry):
        _row_copy(ys_hbm, 0, y_buf.at[0], 0, sem).wait()
        _row_copy(ys_hbm, 0, y_buf.at[1], 0, sem).wait()
        return carry

    lax.fori_loop(0, tl, issue, 0)
    lax.fori_loop(0, tl, drain, 0)
    slab = slab_ref[...]
    out_ref[...] = h_ref[...] + slab[:, 4:5] * y_buf[0] + slab[:, 5:6] * y_buf[1]


def _combine(pad_start, route_i, h, slab, ys):
    t, d = h.shape
    n_tiles = t // TOK_TILE
    grid_spec = pltpu.PrefetchScalarGridSpec(
        num_scalar_prefetch=1,
        grid=(n_tiles,),
        in_specs=[pl.BlockSpec(memory_space=pl.ANY),
                  pl.BlockSpec((TOK_TILE, d), lambda i, st: (i, 0)),
                  pl.BlockSpec((TOK_TILE, PAIR), lambda i, st: (i, 0)),
                  pl.BlockSpec(memory_space=pl.ANY)],
        out_specs=pl.BlockSpec((TOK_TILE, d), lambda i, st: (i, 0)),
        scratch_shapes=[pltpu.SMEM((4 * TOK_TILE,), I32), pltpu.VMEM((2, TOK_TILE, d), F32),
                        pltpu.SemaphoreType.DMA, pltpu.SemaphoreType.DMA],
    )
    return pl.pallas_call(
        _combine_kernel,
        grid_spec=grid_spec,
        out_shape=jax.ShapeDtypeStruct((t, d), F32),
        compiler_params=_cparams(("arbitrary",)),
        name="combine",
    )(pad_start, route_i, h, slab, ys)


def _route_ints(slab):
    t = slab.shape[0]
    return slab[:, 0:4].astype(I32).reshape(t // TOK_TILE, 4 * TOK_TILE)


def _pad_rows(x, n_b, rows):
    s = x.shape[0] // n_b
    x = x.reshape(n_b, s, x.shape[1])
    return jnp.pad(x, ((0, 0), (0, rows - s), (0, 0))).reshape(n_b * rows, -1)


def _layer(xp, xs, cache_k_sb, cache_v_sb, cache_k_mb, cache_v_mb, page_table,
           norm_attn, w_in, q_norm_mb, k_norm_mb, out_norm_sb, out_norm_mb, w_out,
           norm_ffn, w_rg, b_rg, w_re, b_re, w_gate, w_up, w_down):
    n_p, s_p, d = xp.shape
    n_s, s_s, _ = xs.shape
    t_p, t_s = n_p * s_p, n_s * s_s
    t_all = t_p + t_s
    dh = w_in.shape[1] // 6
    n_heads = dh // HEAD_DIM
    assert s_p % MOBA_BLOCK == 0 and dh % PAIR == 0
    xp2, xs2 = xp.reshape(t_p, d), xs.reshape(t_s, d)
    w_in_b = w_in.astype(BF16)
    slopes = 2.0 ** (-8.0 * jnp.arange(1, n_heads + 1, dtype=F32) / n_heads)

    pp = _project(xp2, norm_attn, w_in_b, q_norm_mb, k_norm_mb)
    ps = _project(xs2, norm_attn, w_in_b, q_norm_mb, k_norm_mb)
    (_, k1p, v1p, _, k2p, v2p, q1pb, k1pb, v1pb, q2pb, k2pb, v2pb, kmp) = pp
    (q1s, k1s, v1s, q2s, k2s, v2s) = ps[:6]

    nb = s_p // MOBA_BLOCK
    nbp = -(-nb // PAIR) * PAIR
    kmp = jnp.pad(kmp.reshape(n_p, nb, dh), ((0, 0), (0, nbp - nb), (0, 0))).reshape(n_p * nbp, dh)
    o_sb_p = _sb_prompt(q1pb, k1pb, v1pb, n_p, s_p)
    o_mb_p = _moba_prompt(slopes, q2pb, k2pb, v2pb, kmp, n_p, s_p)

    pool, page = cache_k_sb.shape[0], cache_k_sb.shape[1]
    c_k_sb = cache_k_sb.reshape(pool, page, dh)
    c_v_sb = cache_v_sb.reshape(pool, page, dh)
    c_k_mb = cache_k_mb.reshape(pool, page, dh)
    c_v_mb = cache_v_mb.reshape(pool, page, dh)
    o_sb_s = _sb_sample(page_table, q1s, _pad_rows(k1s, n_s, KB), _pad_rows(v1s, n_s, KB),
                        c_k_sb, c_v_sb)
    n_blk_s = page_table.shape[1] * page // MOBA_BLOCK
    nbp_s = -(-n_blk_s // PAIR) * PAIR
    km_s = _past_block_means(page_table, c_k_mb).reshape(n_s, n_blk_s, dh)
    km_s = jnp.pad(km_s, ((0, 0), (0, nbp_s - n_blk_s), (0, 0)))
    slope_rows = jnp.broadcast_to(jnp.repeat(slopes, s_s)[:, None], (n_heads * s_s, PAIR))
    o_mb_s = _moba_sample(page_table, q2s, slope_rows, km_s, _pad_rows(k2s, n_s, MOBA_BLOCK),
                          _pad_rows(v2s, n_s, MOBA_BLOCK), c_k_mb, c_v_mb)

    w_out_b = w_out.astype(BF16)
    w_route = jnp.pad(jnp.concatenate([w_rg, w_re], axis=1), ((0, 0), (0, PAIR - N_GROUPS - N_EXPERTS)))
    b_route = jnp.pad(jnp.concatenate([b_rg, b_re]), (0, PAIR - N_GROUPS - N_EXPERTS))[None]
    cnt0 = jnp.zeros((1, PAIR), F32)
    hp, xnp_, slab_p, cnt1 = _mix_route(cnt0, o_sb_p, o_mb_p, xp2, out_norm_sb, out_norm_mb,
                                        w_out_b, norm_ffn, w_route, b_route)
    hs, xns, slab_s, cnt2 = _mix_route(cnt1, o_sb_s, o_mb_s, xs2, out_norm_sb, out_norm_mb,
                                       w_out_b, norm_ffn, w_route, b_route)

    counts = cnt2[0, ROUTE_LANE0:ROUTE_LANE0 + N_EXPERTS].astype(I32)
    padded = (counts + SLOT_BLOCK - 1) // SLOT_BLOCK * SLOT_BLOCK
    pad_end = jnp.cumsum(padded)
    pad_start = (pad_end - padded).astype(I32)
    n_blocks = -(-2 * t_all // SLOT_BLOCK) + N_EXPERTS
    block_expert = jnp.minimum(
        jnp.searchsorted(pad_end, jnp.arange(n_blocks, dtype=I32) * SLOT_BLOCK, side='right'),
        N_EXPERTS - 1).astype(I32)

    route_p, route_s = _route_ints(slab_p), _route_ints(slab_s)
    x_sorted = jnp.zeros((n_blocks * SLOT_BLOCK, d), F32)
    x_sorted = _dispatch(pad_start, route_p, xnp_, x_sorted)
    x_sorted = _dispatch(pad_start, route_s, xns, x_sorted)
    y_sorted = _experts(block_expert, x_sorted, w_gate.astype(BF16), w_up.astype(BF16),
                        w_down.astype(BF16))
    yp = _combine(pad_start, route_p, hp, slab_p, y_sorted)
    ys = _combine(pad_start, route_s, hs, slab_s, y_sorted)

    h4 = lambda a, n, s: a.reshape(n, s, n_heads, HEAD_DIM)
    new_kv = (h4(k1p, n_p, s_p), h4(v1p, n_p, s_p), h4(k2p, n_p, s_p), h4(v2p, n_p, s_p),
              h4(k1s, n_s, s_s), h4(v1s, n_s, s_s), h4(k2s, n_s, s_s), h4(v2s, n_s, s_s))
    return yp.reshape(n_p, s_p, d), ys.reshape(n_s, s_s, d), new_kv


def kernel(x_prompt, x_sample, cache_k_sb, cache_v_sb, cache_k_mb, cache_v_mb, page_table, norm_attn, w_in, q_norm_mb, k_norm_mb, out_norm_sb, out_norm_mb, w_out, norm_ffn, w_router_group, b_router_group, w_router_expert, b_router_expert, w_gate, w_up, w_down):
    depth = w_in.shape[0]
    xp, xs = x_prompt, x_sample
    kv = []
    for l in range(depth):
        xp, xs, new_kv = _layer(
            xp, xs, cache_k_sb[l], cache_v_sb[l], cache_k_mb[l], cache_v_mb[l], page_table,
            norm_attn[l], w_in[l], q_norm_mb[l], k_norm_mb[l], out_norm_sb[l], out_norm_mb[l],
            w_out[l], norm_ffn[l], w_router_group[l], b_router_group[l], w_router_expert[l],
            b_router_expert[l], w_gate[l], w_up[l], w_down[l])
        kv.append(new_kv)
    stacked = tuple(jnp.stack([layer_kv[i] for layer_kv in kv]) for i in range(8))
    return (xp, xs) + stacked
```

```python
import functools

import jax
import jax.numpy as jnp
from jax import lax
from jax.experimental import pallas as pl
from jax.experimental.pallas import tpu as pltpu

F32 = jnp.float32
BF16 = jnp.bfloat16
I32 = jnp.int32

HEAD_DIM = 64
PAIR = 128
MOBA_BLOCK = 256
MOBA_TOPK = 3
N_GROUPS = 4
EXPERTS_PER_GROUP = 8
N_EXPERTS = N_GROUPS * EXPERTS_PER_GROUP
RMS_EPS = 1e-6
SCALE = HEAD_DIM ** -0.5
QB = 128
KB = 128
TOK_TILE = 256
SLOT_BLOCK = 128
ROUTE_LANE0 = N_GROUPS
SB_SKIP = 110.0
NEG = -1e30
VMEM_LIMIT = 56 * 1024 * 1024


def _cparams(sem):
    return pltpu.CompilerParams(dimension_semantics=sem, vmem_limit_bytes=VMEM_LIMIT)


def _split_bf16(x):
    hi = x.astype(BF16)
    lo = (x - hi.astype(F32)).astype(BF16)
    return hi, lo


def _dot(a, b):
    return jnp.dot(a, b, preferred_element_type=F32)


def _dot_nt(a, b):
    return lax.dot_general(a, b, (((1,), (1,)), ((), ())), preferred_element_type=F32)


def _proj_kernel(x_ref, g_ref, w_ref, qn_ref, kn_ref, seg_ref,
                 k1t_ref, v1t_ref, k2t_ref, v2t_ref, q1s_ref, q2s_ref,
                 q1b_ref, k1b_ref, v1b_ref, q2b_ref, k2b_ref, v2b_ref, v2tb_ref, km_ref):
    x = x_ref[...]
    tl = x.shape[0]
    ms = jnp.mean(x * x, axis=-1, keepdims=True)
    xn = x * lax.rsqrt(ms + RMS_EPS) * g_ref[...]
    proj = _dot(xn.astype(BF16), w_ref[...])
    dh = q1s_ref.shape[1]
    seg = seg_ref[...]

    def headnorm(t, g):
        hi, lo = _split_bf16(t * t)
        ssum = _dot(hi, seg) + _dot(lo, seg)
        return t * lax.rsqrt(ssum * (1.0 / HEAD_DIM) + RMS_EPS) * g

    q1 = proj[:, 0 * dh:1 * dh]
    k1 = proj[:, 1 * dh:2 * dh]
    v1 = proj[:, 2 * dh:3 * dh]
    q2 = headnorm(proj[:, 3 * dh:4 * dh], qn_ref[...])
    k2 = headnorm(proj[:, 4 * dh:5 * dh], kn_ref[...])
    v2 = proj[:, 5 * dh:6 * dh]
    v2t = v2.T
    for ref, val in ((k1t_ref, k1.T), (v1t_ref, v1.T), (k2t_ref, k2.T), (v2t_ref, v2t)):
        ref[0] = val
    q1s = q1 * SCALE
    q2s = q2 * SCALE
    q1s_ref[...] = q1s
    q2s_ref[...] = q2s
    for bref, val in ((q1b_ref, q1s), (k1b_ref, k1), (v1b_ref, v1),
                      (q2b_ref, q2s), (k2b_ref, k2), (v2b_ref, v2)):
        bref[...] = val.astype(BF16)
    v2tb_ref[...] = v2t.astype(BF16)
    for c in range(tl // MOBA_BLOCK):
        blk = k2[c * MOBA_BLOCK:(c + 1) * MOBA_BLOCK]
        km_ref[c] = jnp.sum(blk, axis=0, keepdims=True) * (1.0 / MOBA_BLOCK)


def _project(x, n_seq, norm_g, w_in_b, q_norm, k_norm):
    t, d = x.shape
    s = t // n_seq
    dh = w_in_b.shape[1] // 6
    n_heads = dh // HEAD_DIM
    assert s % TOK_TILE == 0 and TOK_TILE % MOBA_BLOCK == 0
    tps = s // TOK_TILE
    r = jnp.arange(dh) // HEAD_DIM
    seg = (r[:, None] == r[None, :]).astype(BF16)
    qn = jnp.tile(q_norm, n_heads)[None]
    kn = jnp.tile(k_norm, n_heads)[None]
    tile = lambda i: (i, 0)
    const = lambda i: (0, 0)
    t_spec = pl.BlockSpec((1, dh, TOK_TILE), lambda i: (i // tps, 0, i % tps))
    f_spec = pl.BlockSpec((TOK_TILE, dh), tile)
    n_km = TOK_TILE // MOBA_BLOCK
    return pl.pallas_call(
        _proj_kernel,
        grid=(t // TOK_TILE,),
        in_specs=[pl.BlockSpec((TOK_TILE, d), tile), pl.BlockSpec((1, d), const),
                  pl.BlockSpec(w_in_b.shape, const), pl.BlockSpec((1, dh), const),
                  pl.BlockSpec((1, dh), const), pl.BlockSpec((dh, dh), const)],
        out_specs=[t_spec] * 4 + [f_spec] * 8
        + [pl.BlockSpec((dh, TOK_TILE), lambda i: (0, i)),
           pl.BlockSpec((n_km, 1, dh), lambda i: (i, 0, 0))],
        out_shape=[jax.ShapeDtypeStruct((n_seq, dh, s), F32)] * 4
        + [jax.ShapeDtypeStruct((t, dh), F32)] * 2
        + [jax.ShapeDtypeStruct((t, dh), BF16)] * 6
        + [jax.ShapeDtypeStruct((dh, t), BF16),
           jax.ShapeDtypeStruct((t // MOBA_BLOCK, 1, dh), F32)],
        compiler_params=_cparams(("parallel",)),
        name="in_proj",
    )(x, norm_g[None], w_in_b, qn, kn, seg)


def _sb_weights(z, carry, cum_u, valid):
    lk = -(jnp.maximum(z, 0.0) + jnp.log1p(jnp.exp(-jnp.abs(z))))
    if valid is not None:
        lk = jnp.where(valid, lk, 0.0)
    hi, lo = _split_bf16(lk)
    cs = _dot(hi, cum_u) + _dot(lo, cum_u)
    a = jnp.exp(z + cs + carry)
    if valid is not None:
        a = jnp.where(valid, a, 0.0)
    return a, carry + cs[:, 0:1]


def _sb_block(qh, k, v, carry, acc, cum_u, valid):
    a, carry = _sb_weights(_dot_nt(qh, k), carry, cum_u, valid)
    return carry, acc + _dot(a.astype(BF16), v)


def _cum_matrix(n):
    r = lax.broadcasted_iota(I32, (n, n), 0)
    c = lax.broadcasted_iota(I32, (n, n), 1)
    return (r >= c).astype(BF16)


def _sb_prompt_kernel(q_ref, k_ref, v_ref, o_ref):
    qb = pl.program_id(2)
    q = q_ref[...]
    lane = lax.broadcasted_iota(I32, (1, PAIR), 1)
    head0 = lane < HEAD_DIM
    zero = jnp.zeros_like(q)
    qh = (jnp.where(head0, q, zero), jnp.where(head0, zero, q))
    cum_u = _cum_matrix(KB)
    r = lax.broadcasted_iota(I32, (QB, KB), 0)
    c = lax.broadcasted_iota(I32, (QB, KB), 1)
    strictly_before = c < r

    def load(kb):
        start = pl.multiple_of(kb * KB, KB)
        return k_ref[pl.ds(start, KB), :], v_ref[pl.ds(start, KB), :]

    k, v = load(qb)
    carries, accs = [], []
    for h in range(2):
        cr, ac = _sb_block(qh[h], k, v, jnp.zeros((QB, 1), F32), jnp.zeros((QB, PAIR), F32),
                           cum_u, strictly_before)
        carries.append(cr)
        accs.append(ac)

    def top(c0, c1):
        return jnp.max(jnp.maximum(c0, c1))

    def cond(st):
        kb, cmax = st[0], st[1]
        return jnp.logical_and(kb >= 0, cmax > -SB_SKIP)

    def body(st):
        kb, _, c0, c1, a0, a1 = st
        k, v = load(kb)
        c0, a0 = _sb_block(qh[0], k, v, c0, a0, cum_u, None)
        c1, a1 = _sb_block(qh[1], k, v, c1, a1, cum_u, None)
        return kb - 1, top(c0, c1), c0, c1, a0, a1

    st = lax.while_loop(cond, body, (qb - 1, top(carries[0], carries[1]),
                                     carries[0], carries[1], accs[0], accs[1]))
    o_ref[...] = jnp.where(head0, st[4], st[5])


def _sb_prompt(q1b, k1b, v1b, n_seq, seq_len):
    t_all, dh = q1b.shape
    n_pairs = dh // PAIR
    n_qb = seq_len // QB
    return pl.pallas_call(
        _sb_prompt_kernel,
        grid=(n_seq, n_pairs, n_qb),
        in_specs=[pl.BlockSpec((QB, PAIR), lambda n, j, i: (n * n_qb + i, j)),
                  pl.BlockSpec((seq_len, PAIR), lambda n, j, i: (n, j)),
                  pl.BlockSpec((seq_len, PAIR), lambda n, j, i: (n, j))],
        out_specs=pl.BlockSpec((QB, PAIR), lambda n, j, i: (n * n_qb + i, j)),
        out_shape=jax.ShapeDtypeStruct((t_all, dh), F32),
        compiler_params=_cparams(("parallel", "parallel", "arbitrary")),
        name="sb_prompt",
    )(q1b, k1b, v1b)


def _top_blocks(gate, n_past, blk_iota, axis):
    g = jnp.where(blk_iota < n_past, gate, NEG)
    big = jnp.int32(2 ** 30)
    sel = jnp.zeros(gate.shape, F32)
    for r in range(MOBA_TOPK):
        m = jnp.max(g, axis=axis, keepdims=True)
        idx = jnp.min(jnp.where(g == m, blk_iota, big), axis=axis, keepdims=True)
        pick = jnp.logical_and(blk_iota == idx, r < n_past)
        sel = jnp.where(pick, 1.0, sel)
        g = jnp.where(pick, 2.0 * NEG, g)
    return sel


def _softmax_step_t(s_t, v_t, m, l, acc_t):
    m_new = jnp.maximum(m, jnp.max(s_t, axis=0, keepdims=True))
    alpha = jnp.exp(m - m_new)
    p = jnp.exp(s_t - m_new)
    l = alpha * l + jnp.sum(p, axis=0, keepdims=True)
    acc_t = alpha * acc_t + _dot(v_t, p.astype(BF16))
    return m_new, l, acc_t


def _moba_prompt_kernel(slopes_ref, q_ref, k_ref, vt_ref, km_ref, o_ref, sel_ref, cb_ref):
    j = pl.program_id(1)
    qb = pl.program_id(2)
    q = q_ref[...]
    lane = lax.broadcasted_iota(I32, (1, PAIR), 1)
    zero = jnp.zeros_like(q)
    qh = (jnp.where(lane < HEAD_DIM, q, zero), jnp.where(lane < HEAD_DIM, zero, q))
    own = (qb * QB) // MOBA_BLOCK
    q_off = qb * QB - own * MOBA_BLOCK
    nb = km_ref.shape[0]
    blk_iota = lax.broadcasted_iota(I32, (nb, QB), 0)
    km_hi, km_lo = _split_bf16(km_ref[...])
    key = lax.broadcasted_iota(I32, (MOBA_BLOCK, QB), 0)
    qry = lax.broadcasted_iota(I32, (MOBA_BLOCK, QB), 1)
    slopes = []
    for h in range(2):
        gate_t = _dot_nt(km_hi, qh[h]) + _dot_nt(km_lo, qh[h])
        sel_ref[h] = _top_blocks(gate_t, own, blk_iota, 0)
        slopes.append(slopes_ref[2 * j + h])
        cb_ref[h] = slopes[h] * key.astype(F32)

    def load(n):
        start = pl.multiple_of(n * MOBA_BLOCK, MOBA_BLOCK)
        return k_ref[pl.ds(start, MOBA_BLOCK), :], vt_ref[:, pl.ds(start, MOBA_BLOCK)]

    def body(n, st):
        k, vt = load(n)
        base = (n * MOBA_BLOCK - qb * QB).astype(F32)
        qk = [_dot_nt(k, qh[h]) for h in range(2)]
        out = []
        for h in range(2):
            m, l, acc = st[3 * h:3 * h + 3]
            row = jnp.where(sel_ref[h, pl.ds(n, 1), :] > 0.0, slopes[h] * base, NEG)
            s_t = qk[h] + cb_ref[h] + row
            out.extend(_softmax_step_t(s_t, vt[h * HEAD_DIM:(h + 1) * HEAD_DIM], m, l, acc))
        return tuple(out)

    init = (jnp.full((1, QB), NEG, F32), jnp.zeros((1, QB), F32), jnp.zeros((HEAD_DIM, QB), F32)) * 2
    st = lax.fori_loop(0, own, body, init)
    k, vt = load(own)
    causal = key <= qry + q_off
    res = []
    for h in range(2):
        m, l, acc = st[3 * h:3 * h + 3]
        s_t = _dot_nt(k, qh[h]) + cb_ref[h] - slopes[h] * q_off.astype(F32)
        s_t = jnp.where(causal, s_t, NEG)
        m, l, acc = _softmax_step_t(s_t, vt[h * HEAD_DIM:(h + 1) * HEAD_DIM], m, l, acc)
        res.append(acc / l)
    o_ref[...] = jnp.concatenate(res, axis=0).T


def _moba_prompt(slopes, q2b, k2b, v2t, km, n_seq, seq_len):
    t_all, dh = q2b.shape
    n_pairs = dh // PAIR
    n_qb = seq_len // QB
    nb = km.shape[0] // n_seq
    return pl.pallas_call(
        _moba_prompt_kernel,
        grid=(n_seq, n_pairs, n_qb),
        in_specs=[pl.BlockSpec(memory_space=pltpu.SMEM),
                  pl.BlockSpec((QB, PAIR), lambda n, j, i: (n * n_qb + i, j)),
                  pl.BlockSpec((seq_len, PAIR), lambda n, j, i: (n, j)),
                  pl.BlockSpec((PAIR, seq_len), lambda n, j, i: (j, n)),
                  pl.BlockSpec((nb, PAIR), lambda n, j, i: (n, j))],
        out_specs=pl.BlockSpec((QB, PAIR), lambda n, j, i: (n * n_qb + i, j)),
        out_shape=jax.ShapeDtypeStruct((t_all, dh), F32),
        scratch_shapes=[pltpu.VMEM((2, nb, QB), F32), pltpu.VMEM((2, MOBA_BLOCK, QB), F32)],
        compiler_params=_cparams(("parallel", "parallel", "arbitrary")),
        name="moba_prompt",
    )(slopes, q2b, k2b, v2t, km)


def _head_rows(q, n_heads):
    s, dh = q.shape
    rows = jnp.concatenate([q] * n_heads, axis=0)
    rr = lax.broadcasted_iota(I32, (n_heads * s, dh), 0) // s
    ll = lax.broadcasted_iota(I32, (n_heads * s, dh), 1) // HEAD_DIM
    keep = rr == ll
    return jnp.where(keep, rows, 0.0), keep


def _merge_head_rows(acc, keep, n_heads):
    s = acc.shape[0] // n_heads
    masked = jnp.where(keep, acc, 0.0)
    out = masked[0:s]
    for h in range(1, n_heads):
        out = out + masked[h * s:(h + 1) * s]
    return out


def _sb_sample_kernel(pt_ref, q_ref, kn_ref, vn_ref, kc_hbm, vc_hbm, o_ref, kbuf, vbuf, sem,
                      *, n_heads, n_pages):
    b = pl.program_id(0)
    s, dh = q_ref.shape

    def slot_of(p):
        return (n_pages - 1 - p) % 2

    def copies(p):
        pg = pt_ref[b, p]
        slot = slot_of(p)
        return (pltpu.make_async_copy(kc_hbm.at[pg], kbuf.at[slot], sem.at[0, slot]),
                pltpu.make_async_copy(vc_hbm.at[pg], vbuf.at[slot], sem.at[1, slot]))

    def start(p):
        for cp in copies(p):
            cp.start()

    def wait(p):
        for cp in copies(p):
            cp.wait()

    start(n_pages - 1)
    start(n_pages - 2)
    qrows, keep = _head_rows(q_ref[...], n_heads)
    qrows = qrows.astype(BF16)
    rows = n_heads * s
    cum_u = _cum_matrix(KB)
    r = lax.broadcasted_iota(I32, (rows, KB), 0) % s
    c = lax.broadcasted_iota(I32, (rows, KB), 1)
    carry, acc = _sb_block(qrows, kn_ref[...], vn_ref[...], jnp.zeros((rows, 1), F32),
                           jnp.zeros((rows, dh), F32), cum_u, c < r)

    def cond(st):
        return jnp.logical_and(st[0] >= 0, st[1] > -SB_SKIP)

    def body(st):
        p, _, carry, acc = st
        wait(p)
        slot = slot_of(p)
        a, carry = _sb_weights(_dot(qrows, kbuf[slot].astype(BF16)), carry, cum_u, None)
        acc = acc + _dot_nt(a.astype(BF16), vbuf[slot].astype(BF16))

        @pl.when(p >= 2)
        def _():
            start(p - 2)

        return p - 1, jnp.max(carry), carry, acc

    st = lax.while_loop(cond, body, (jnp.int32(n_pages - 1), jnp.max(carry), carry, acc))
    p_exit = st[0]

    @pl.when(p_exit >= 0)
    def _():
        wait(p_exit)

    @pl.when(p_exit >= 1)
    def _():
        wait(p_exit - 1)

    o_ref[...] = _merge_head_rows(st[3], keep, n_heads)


def _sb_sample(page_table, q1s, k_new, v_new, cache_kt, cache_vt):
    n_b, n_pages = page_table.shape
    dh = q1s.shape[1]
    s = q1s.shape[0] // n_b
    page = cache_kt.shape[2]
    assert page == KB and n_pages >= 2
    n_heads = dh // HEAD_DIM
    per_b = lambda b, pt: (b, 0)
    grid_spec = pltpu.PrefetchScalarGridSpec(
        num_scalar_prefetch=1,
        grid=(n_b,),
        in_specs=[pl.BlockSpec((s, dh), per_b), pl.BlockSpec((KB, dh), per_b),
                  pl.BlockSpec((KB, dh), per_b), pl.BlockSpec(memory_space=pl.ANY),
                  pl.BlockSpec(memory_space=pl.ANY)],
        out_specs=pl.BlockSpec((s, dh), per_b),
        scratch_shapes=[pltpu.VMEM((2, dh, page), F32), pltpu.VMEM((2, dh, page), F32),
                        pltpu.SemaphoreType.DMA((2, 2))],
    )
    return pl.pallas_call(
        functools.partial(_sb_sample_kernel, n_heads=n_heads, n_pages=n_pages),
        grid_spec=grid_spec,
        out_shape=jax.ShapeDtypeStruct(q1s.shape, F32),
        compiler_params=_cparams(("arbitrary",)),
        name="sb_sample",
    )(page_table, q1s, k_new, v_new, cache_kt, cache_vt)


def _moba_pick_kernel(pt_ref, q_ref, k0_ref, k1_ref, sel_ref, gate_ref, *, n_heads):
    del pt_ref
    m = pl.program_id(1)
    n_blk = pl.num_programs(1)
    qrows, _ = _head_rows(q_ref[...], n_heads)
    page = k0_ref.shape[2]
    ksum = (jnp.sum(k0_ref[0], axis=1, keepdims=True) + jnp.sum(k1_ref[0], axis=1, keepdims=True))
    km = jnp.broadcast_to(ksum * (1.0 / MOBA_BLOCK), (ksum.shape[0], page))
    q_hi, q_lo = _split_bf16(qrows)
    k_hi, k_lo = _split_bf16(km)
    g = _dot(q_hi, k_hi) + _dot(q_hi, k_lo) + _dot(q_lo, k_hi)
    blk_iota = lax.broadcasted_iota(I32, gate_ref.shape, 1)

    @pl.when(m == 0)
    def _():
        gate_ref[...] = jnp.zeros(gate_ref.shape, F32)

    for c in range(gate_ref.shape[1] // page):
        cols = slice(c * page, (c + 1) * page)
        gate_ref[:, cols] = jnp.where(blk_iota[:, cols] == m, g, gate_ref[:, cols])

    @pl.when(m == n_blk - 1)
    def _():
        sel_ref[0] = _top_blocks(gate_ref[...], n_blk, blk_iota, 1)


def _moba_pick(page_table, q2s, cache_kt, nbp):
    n_b, n_pages = page_table.shape
    dh = q2s.shape[1]
    s = q2s.shape[0] // n_b
    page = cache_kt.shape[2]
    assert 2 * page == MOBA_BLOCK and nbp % page == 0
    n_heads = dh // HEAD_DIM
    rows = n_heads * s
    n_blk = n_pages // 2
    grid_spec = pltpu.PrefetchScalarGridSpec(
        num_scalar_prefetch=1,
        grid=(n_b, n_blk),
        in_specs=[pl.BlockSpec((s, dh), lambda b, m, pt: (b, 0)),
                  pl.BlockSpec((1, dh, page), lambda b, m, pt: (pt[b, 2 * m], 0, 0)),
                  pl.BlockSpec((1, dh, page), lambda b, m, pt: (pt[b, 2 * m + 1], 0, 0))],
        out_specs=pl.BlockSpec((1, rows, nbp), lambda b, m, pt: (b, 0, 0)),
        scratch_shapes=[pltpu.VMEM((rows, nbp), F32)],
    )
    return pl.pallas_call(
        functools.partial(_moba_pick_kernel, n_heads=n_heads),
        grid_spec=grid_spec,
        out_shape=jax.ShapeDtypeStruct((n_b, rows, nbp), F32),
        compiler_params=_cparams(("parallel", "arbitrary")),
        name="moba_pick",
    )(page_table, q2s, cache_kt, cache_kt)


def _moba_sample_kernel(pt_ref, q_ref, slope_ref, sel_ref, kn_ref, vn_ref,
                        k0_ref, k1_ref, v0_ref, v1_ref, o_ref, m_ref, l_ref, acc_ref,
                        *, n_heads, past_len):
    del pt_ref
    blk = pl.program_id(1)
    n_blk = pl.num_programs(1) - 1
    s = q_ref.shape[0]
    rows = n_heads * s
    qrows, keep = _head_rows(q_ref[...], n_heads)
    qrows = qrows.astype(BF16)
    nbp = sel_ref.shape[2]
    page = k0_ref.shape[2]
    blk_iota = lax.broadcasted_iota(I32, (rows, nbp), 1)
    slope = slope_ref[:, 0:1]
    r = lax.broadcasted_iota(I32, (rows, MOBA_BLOCK), 0) % s
    c = lax.broadcasted_iota(I32, (rows, MOBA_BLOCK), 1)

    @pl.when(blk == 0)
    def _():
        m_ref[...] = jnp.full(m_ref.shape, NEG, F32)
        l_ref[...] = jnp.zeros(l_ref.shape, F32)
        acc_ref[...] = jnp.zeros(acc_ref.shape, F32)

    def update(qk, ok, dist0, pv):
        sc = qk - slope * ((r - c).astype(F32) + dist0)
        sc = jnp.where(ok, sc, NEG)
        m = m_ref[:, 0:1]
        m_new = jnp.maximum(m, jnp.max(sc, axis=-1, keepdims=True))
        alpha = jnp.exp(m - m_new)
        p = jnp.where(ok, jnp.exp(sc - m_new), 0.0)
        l = alpha * l_ref[:, 0:1] + jnp.sum(p, axis=-1, keepdims=True)
        acc = alpha * acc_ref[...] + pv(p.astype(BF16))
        m_ref[...] = jnp.broadcast_to(m_new, m_ref.shape)
        l_ref[...] = jnp.broadcast_to(l, l_ref.shape)
        acc_ref[...] = acc
        return l, acc

    @pl.when(blk < n_blk)
    def _():
        qk = jnp.concatenate([_dot(qrows, k0_ref[0].astype(BF16)), _dot(qrows, k1_ref[0].astype(BF16))],
                             axis=1)
        picked = jnp.max(jnp.where(blk_iota == blk, sel_ref[0], 0.0), axis=-1, keepdims=True)
        ok = jnp.broadcast_to(picked, (rows, MOBA_BLOCK)) > 0.0
        dist0 = (past_len - blk * MOBA_BLOCK).astype(F32)
        update(qk, ok, dist0,
               lambda p: _dot_nt(p[:, 0:page], v0_ref[0].astype(BF16))
               + _dot_nt(p[:, page:2 * page], v1_ref[0].astype(BF16)))

    @pl.when(blk == n_blk)
    def _():
        l, acc = update(_dot_nt(qrows, kn_ref[...]), c <= r, jnp.float32(0.0),
                        lambda p: _dot(p, vn_ref[...]))
        o_ref[...] = _merge_head_rows(acc / l, keep, n_heads)


def _moba_sample(page_table, q2s, slope_rows, sel, k_new, v_new, cache_kt, cache_vt):
    n_b, n_pages = page_table.shape
    dh = q2s.shape[1]
    s = q2s.shape[0] // n_b
    page = cache_kt.shape[2]
    n_blk = n_pages // 2
    n_heads = dh // HEAD_DIM
    rows = n_heads * s
    nbp = sel.shape[2]
    last = n_blk - 1

    def pg(off):
        return lambda b, m, pt: (pt[b, 2 * jnp.minimum(m, last) + off], 0, 0)

    per_b = lambda b, m, pt: (b, 0)
    grid_spec = pltpu.PrefetchScalarGridSpec(
        num_scalar_prefetch=1,
        grid=(n_b, n_blk + 1),
        in_specs=[pl.BlockSpec((s, dh), per_b),
                  pl.BlockSpec((rows, PAIR), lambda b, m, pt: (0, 0)),
                  pl.BlockSpec((1, rows, nbp), lambda b, m, pt: (b, 0, 0)),
                  pl.BlockSpec((MOBA_BLOCK, dh), per_b),
                  pl.BlockSpec((MOBA_BLOCK, dh), per_b),
                  pl.BlockSpec((1, dh, page), pg(0)), pl.BlockSpec((1, dh, page), pg(1)),
                  pl.BlockSpec((1, dh, page), pg(0)), pl.BlockSpec((1, dh, page), pg(1))],
        out_specs=pl.BlockSpec((s, dh), per_b),
        scratch_shapes=[pltpu.VMEM((rows, PAIR), F32), pltpu.VMEM((rows, PAIR), F32),
                        pltpu.VMEM((rows, dh), F32)],
    )
    return pl.pallas_call(
        functools.partial(_moba_sample_kernel, n_heads=n_heads, past_len=n_pages * page),
        grid_spec=grid_spec,
        out_shape=jax.ShapeDtypeStruct(q2s.shape, F32),
        compiler_params=_cparams(("parallel", "arbitrary")),
        name="moba_sample",
    )(page_table, q2s, slope_rows, sel, k_new, v_new, cache_kt, cache_kt, cache_vt, cache_vt)


def _mix_route_kernel(cnt_in_ref, o1_ref, o2_ref, x_ref, g1_ref, g2_ref, wo_ref, gf_ref,
                      wr_ref, br_ref, h_ref, xn_ref, slab_ref, cnt_ref, run_ref):
    i = pl.program_id(0)

    @pl.when(i == 0)
    def _():
        run_ref[...] = cnt_in_ref[...]

    def rms(t, g):
        return t * lax.rsqrt(jnp.mean(t * t, axis=-1, keepdims=True) + RMS_EPS) * g

    n1 = rms(o1_ref[...], g1_ref[...]).astype(BF16)
    n2 = rms(o2_ref[...], g2_ref[...]).astype(BF16)
    dh = n1.shape[1]
    h = x_ref[...] + _dot(n1, wo_ref[0:dh, :]) + _dot(n2, wo_ref[dh:2 * dh, :])
    h_ref[...] = h
    xn = rms(h, gf_ref[...])
    xn_ref[...] = xn

    x_hi, x_lo = _split_bf16(xn)
    w_hi, w_lo = _split_bf16(wr_ref[...])
    logits = _dot(x_hi, w_hi) + _dot(x_hi, w_lo) + _dot(x_lo, w_hi) + br_ref[...]
    tl, nl = logits.shape
    lane = lax.broadcasted_iota(I32, (tl, nl), 1)
    big = jnp.int32(2 ** 30)

    def first_max(vals):
        m = jnp.max(vals, axis=-1, keepdims=True)
        idx = jnp.min(jnp.where(vals == m, lane, big), axis=-1, keepdims=True)
        return m, idx

    grp_logit = jnp.where(lane < N_GROUPS, logits, NEG)
    gm, grp = first_max(grp_logit)
    denom = jnp.sum(jnp.where(lane < N_GROUPS, jnp.exp(grp_logit - gm), 0.0), axis=-1, keepdims=True)
    p_top = 1.0 / denom
    e_lo = ROUTE_LANE0 + grp * EXPERTS_PER_GROUP
    in_grp = jnp.logical_and(lane >= e_lo, lane < e_lo + EXPERTS_PER_GROUP)
    e_logit = jnp.where(in_grp, logits, NEG)
    m1, i1 = first_max(e_logit)
    m2, i2 = first_max(jnp.where(lane == i1, 2.0 * NEG, e_logit))
    e2 = jnp.exp(m2 - m1)
    gate1 = p_top / (1.0 + e2)
    gate2 = p_top * e2 / (1.0 + e2)

    hot1 = lane == i1
    hot2 = lane == i2
    both = jnp.logical_or(hot1, hot2)
    rr = lax.broadcasted_iota(I32, (tl, tl), 0)
    cc = lax.broadcasted_iota(I32, (tl, tl), 1)
    earlier = (cc < rr).astype(BF16)
    before = _dot(earlier, jnp.where(both, 1.0, 0.0).astype(BF16)) + run_ref[...]
    rank1 = jnp.sum(jnp.where(hot1, before, 0.0), axis=-1, keepdims=True)
    rank2 = jnp.sum(jnp.where(hot2, before, 0.0), axis=-1, keepdims=True)
    run_ref[...] = run_ref[...] + jnp.sum(jnp.where(both, 1.0, 0.0), axis=0, keepdims=True)
    cnt_ref[...] = run_ref[...]

    eid1 = (i1 - ROUTE_LANE0).astype(F32)
    eid2 = (i2 - ROUTE_LANE0).astype(F32)
    cols = (eid1, eid2, rank1, rank2, gate1, gate2)
    slab = jnp.zeros((tl, nl), F32)
    for n, col in enumerate(cols):
        slab = jnp.where(lane == n, col, slab)
    slab_ref[...] = slab


def _mix_route(cnt_in, o1, o2, x, g1, g2, w_out_b, g_ffn, w_route, b_route):
    t, d = x.shape
    dh = o1.shape[1]
    assert t % TOK_TILE == 0
    tile = lambda i: (i, 0)
    const = lambda i: (0, 0)
    return pl.pallas_call(
        _mix_route_kernel,
        grid=(t // TOK_TILE,),
        in_specs=[pl.BlockSpec((1, PAIR), const),
                  pl.BlockSpec((TOK_TILE, dh), tile),
                  pl.BlockSpec((TOK_TILE, dh), tile),
                  pl.BlockSpec((TOK_TILE, d), tile),
                  pl.BlockSpec((1, dh), const), pl.BlockSpec((1, dh), const),
                  pl.BlockSpec((d, d), const), pl.BlockSpec((1, d), const),
                  pl.BlockSpec((d, PAIR), const), pl.BlockSpec((1, PAIR), const)],
        out_specs=[pl.BlockSpec((TOK_TILE, d), tile), pl.BlockSpec((TOK_TILE, d), tile),
                   pl.BlockSpec((TOK_TILE, PAIR), tile), pl.BlockSpec((1, PAIR), const)],
        out_shape=[jax.ShapeDtypeStruct((t, d), F32), jax.ShapeDtypeStruct((t, d), F32),
                   jax.ShapeDtypeStruct((t, PAIR), F32), jax.ShapeDtypeStruct((1, PAIR), F32)],
        scratch_shapes=[pltpu.VMEM((1, PAIR), F32)],
        compiler_params=_cparams(("arbitrary",)),
        name="mix_route",
    )(cnt_in, o1, o2, x, g1[None], g2[None], w_out_b, g_ffn[None], w_route, b_route)


def _row_copy(src, src_row, dst, dst_row, sem):
    return pltpu.make_async_copy(src.at[pl.ds(src_row, 1)], dst.at[pl.ds(dst_row, 1)], sem)


def _load_route(route_hbm, route_smem, sem):
    cp = pltpu.make_async_copy(route_hbm.at[pl.program_id(0)], route_smem, sem)
    cp.start()
    cp.wait()


def _dispatch_kernel(start_ref, route_hbm, xn_ref, xs_in, xs_out, route_smem, rsem, sem):
    del xs_in
    _load_route(route_hbm, route_smem, rsem)
    tl = xn_ref.shape[0]

    def slot(i, k):
        return start_ref[route_smem[4 * i + k]] + route_smem[4 * i + 2 + k]

    def issue(i, carry):
        _row_copy(xn_ref, i, xs_out, slot(i, 0), sem).start()
        _row_copy(xn_ref, i, xs_out, slot(i, 1), sem).start()
        return carry

    def drain(i, carry):
        _row_copy(xn_ref, 0, xs_out, 0, sem).wait()
        _row_copy(xn_ref, 0, xs_out, 0, sem).wait()
        return carry

    lax.fori_loop(0, tl, issue, 0)
    lax.fori_loop(0, tl, drain, 0)


def _dispatch(pad_start, route_i, xn, xs):
    t, d = xn.shape
    n_tiles = t // TOK_TILE
    grid_spec = pltpu.PrefetchScalarGridSpec(
        num_scalar_prefetch=1,
        grid=(n_tiles,),
        in_specs=[pl.BlockSpec(memory_space=pl.ANY),
                  pl.BlockSpec((TOK_TILE, d), lambda i, st: (i, 0)),
                  pl.BlockSpec(memory_space=pl.ANY)],
        out_specs=pl.BlockSpec(memory_space=pl.ANY),
        scratch_shapes=[pltpu.SMEM((4 * TOK_TILE,), I32), pltpu.SemaphoreType.DMA,
                        pltpu.SemaphoreType.DMA],
    )
    return pl.pallas_call(
        _dispatch_kernel,
        grid_spec=grid_spec,
        out_shape=jax.ShapeDtypeStruct(xs.shape, xs.dtype),
        input_output_aliases={3: 0},
        compiler_params=_cparams(("arbitrary",)),
        name="dispatch",
    )(pad_start, route_i, xn, xs)


def _expert_kernel(be_ref, x_ref, wg_ref, wu_ref, wd_ref, y_ref):
    del be_ref
    x = x_ref[...].astype(BF16)
    g = _dot(x, wg_ref[0])
    u = _dot(x, wu_ref[0])
    hid = g * (1.0 / (1.0 + jnp.exp(-g))) * u
    y_ref[...] = _dot(hid.astype(BF16), wd_ref[0])


def _experts(block_expert, xs, wg_b, wu_b, wd_b):
    n_slots, d = xs.shape
    de = wg_b.shape[2]
    n_blocks = n_slots // SLOT_BLOCK
    grid_spec = pltpu.PrefetchScalarGridSpec(
        num_scalar_prefetch=1,
        grid=(n_blocks,),
        in_specs=[pl.BlockSpec((SLOT_BLOCK, d), lambda b, be: (b, 0)),
                  pl.BlockSpec((1, d, de), lambda b, be: (be[b], 0, 0)),
                  pl.BlockSpec((1, d, de), lambda b, be: (be[b], 0, 0)),
                  pl.BlockSpec((1, de, d), lambda b, be: (be[b], 0, 0))],
        out_specs=pl.BlockSpec((SLOT_BLOCK, d), lambda b, be: (b, 0)),
    )
    return pl.pallas_call(
        _expert_kernel,
        grid_spec=grid_spec,
        out_shape=jax.ShapeDtypeStruct((n_slots, d), F32),
        compiler_params=_cparams(("arbitrary",)),
        name="experts",
    )(block_expert, xs, wg_b, wu_b, wd_b)


def _combine_kernel(start_ref, route_hbm, h_ref, slab_ref, ys_hbm, out_ref,
                    route_smem, y_buf, rsem, sem):
    _load_route(route_hbm, route_smem, rsem)
    tl = h_ref.shape[0]

    def slot(i, k):
        return start_ref[route_smem[4 * i + k]] + route_smem[4 * i + 2 + k]

    def issue(i, carry):
        _row_copy(ys_hbm, slot(i, 0), y_buf.at[0], i, sem).start()
        _row_copy(ys_hbm, slot(i, 1), y_buf.at[1], i, sem).start()
        return carry

    def drain(i, carry):
        _row_copy(ys_hbm, 0, y_buf.at[0], 0, sem).wait()
        _row_copy(ys_hbm, 0, y_buf.at[1], 0, sem).wait()
        return carry

    lax.fori_loop(0, tl, issue, 0)
    lax.fori_loop(0, tl, drain, 0)
    slab = slab_ref[...]
    out_ref[...] = h_ref[...] + slab[:, 4:5] * y_buf[0] + slab[:, 5:6] * y_buf[1]


def _combine(pad_start, route_i, h, slab, ys):
    t, d = h.shape
    n_tiles = t // TOK_TILE
    grid_spec = pltpu.PrefetchScalarGridSpec(
        num_scalar_prefetch=1,
        grid=(n_tiles,),
        in_specs=[pl.BlockSpec(memory_space=pl.ANY),
                  pl.BlockSpec((TOK_TILE, d), lambda i, st: (i, 0)),
                  pl.BlockSpec((TOK_TILE, PAIR), lambda i, st: (i, 0)),
                  pl.BlockSpec(memory_space=pl.ANY)],
        out_specs=pl.BlockSpec((TOK_TILE, d), lambda i, st: (i, 0)),
        scratch_shapes=[pltpu.SMEM((4 * TOK_TILE,), I32), pltpu.VMEM((2, TOK_TILE, d), F32),
                        pltpu.SemaphoreType.DMA, pltpu.SemaphoreType.DMA],
    )
    return pl.pallas_call(
        _combine_kernel,
        grid_spec=grid_spec,
        out_shape=jax.ShapeDtypeStruct((t, d), F32),
        compiler_params=_cparams(("arbitrary",)),
        name="combine",
    )(pad_start, route_i, h, slab, ys)


def _route_ints(slab):
    t = slab.shape[0]
    return slab[:, 0:4].astype(I32).reshape(t // TOK_TILE, 4 * TOK_TILE)


def _pad_rows(x, n_b, rows):
    s = x.shape[0] // n_b
    x = x.reshape(n_b, s, x.shape[1])
    return jnp.pad(x, ((0, 0), (0, rows - s), (0, 0))).reshape(n_b * rows, -1)


def _layer(xp, xs, cache_k_sb, cache_v_sb, cache_k_mb, cache_v_mb, page_table,
           norm_attn, w_in, q_norm_mb, k_norm_mb, out_norm_sb, out_norm_mb, w_out,
           norm_ffn, w_rg, b_rg, w_re, b_re, w_gate, w_up, w_down):
    n_p, s_p, d = xp.shape
    n_s, s_s, _ = xs.shape
    t_p, t_s = n_p * s_p, n_s * s_s
    t_all = t_p + t_s
    dh = w_in.shape[1] // 6
    n_heads = dh // HEAD_DIM
    assert s_p % MOBA_BLOCK == 0 and dh % PAIR == 0
    xp2, xs2 = xp.reshape(t_p, d), xs.reshape(t_s, d)
    w_in_b = w_in.astype(BF16)
    slopes = 2.0 ** (-8.0 * jnp.arange(1, n_heads + 1, dtype=F32) / n_heads)

    pp = _project(xp2, n_p, norm_attn, w_in_b, q_norm_mb, k_norm_mb)
    ps = _project(xs2, 1, norm_attn, w_in_b, q_norm_mb, k_norm_mb)
    (k1p, v1p, k2p, v2p, _, _, q1pb, k1pb, v1pb, q2pb, k2pb, _, v2ptb, kmp) = pp
    (k1s, v1s, k2s, v2s, q1s, q2s, _, k1sb, v1sb, _, k2sb, v2sb, _, _) = ps

    nb = s_p // MOBA_BLOCK
    nb8 = -(-nb // 8) * 8
    kmp = jnp.pad(kmp.reshape(n_p, nb, dh), ((0, 0), (0, nb8 - nb), (0, 0))).reshape(n_p * nb8, dh)
    o_sb_p = _sb_prompt(q1pb, k1pb, v1pb, n_p, s_p)
    o_mb_p = _moba_prompt(slopes, q2pb, k2pb, v2ptb, kmp, n_p, s_p)

    pool, page = cache_k_sb.shape[0], cache_k_sb.shape[1]
    as_t = lambda c: c.transpose(0, 2, 3, 1).reshape(pool, dh, page)
    o_sb_s = _sb_sample(page_table, q1s, _pad_rows(k1sb, n_s, KB), _pad_rows(v1sb, n_s, KB),
                        as_t(cache_k_sb), as_t(cache_v_sb))
    c_k_mb, c_v_mb = as_t(cache_k_mb), as_t(cache_v_mb)
    n_blk_s = page_table.shape[1] * page // MOBA_BLOCK
    nbp_s = -(-n_blk_s // PAIR) * PAIR
    sel_s = _moba_pick(page_table, q2s, c_k_mb, nbp_s)
    slope_rows = jnp.broadcast_to(jnp.repeat(slopes, s_s)[:, None], (n_heads * s_s, PAIR))
    o_mb_s = _moba_sample(page_table, q2s, slope_rows, sel_s, _pad_rows(k2sb, n_s, MOBA_BLOCK),
                          _pad_rows(v2sb, n_s, MOBA_BLOCK), c_k_mb, c_v_mb)

    w_out_b = w_out.astype(BF16)
    w_route = jnp.pad(jnp.concatenate([w_rg, w_re], axis=1), ((0, 0), (0, PAIR - N_GROUPS - N_EXPERTS)))
    b_route = jnp.pad(jnp.concatenate([b_rg, b_re]), (0, PAIR - N_GROUPS - N_EXPERTS))[None]
    cnt0 = jnp.zeros((1, PAIR), F32)
    hp, xnp_, slab_p, cnt1 = _mix_route(cnt0, o_sb_p, o_mb_p, xp2, out_norm_sb, out_norm_mb,
                                        w_out_b, norm_ffn, w_route, b_route)
    hs, xns, slab_s, cnt2 = _mix_route(cnt1, o_sb_s, o_mb_s, xs2, out_norm_sb, out_norm_mb,
                                       w_out_b, norm_ffn, w_route, b_route)

    counts = cnt2[0, ROUTE_LANE0:ROUTE_LANE0 + N_EXPERTS].astype(I32)
    padded = (counts + SLOT_BLOCK - 1) // SLOT_BLOCK * SLOT_BLOCK
    pad_end = jnp.cumsum(padded)
    pad_start = (pad_end - padded).astype(I32)
    n_blocks = -(-2 * t_all // SLOT_BLOCK) + N_EXPERTS
    block_row0 = jnp.arange(n_blocks, dtype=I32) * SLOT_BLOCK
    block_expert = jnp.minimum(
        jnp.sum((pad_end[None, :] <= block_row0[:, None]).astype(I32), axis=1), N_EXPERTS - 1)

    route_p, route_s = _route_ints(slab_p), _route_ints(slab_s)
    x_sorted = jnp.zeros((n_blocks * SLOT_BLOCK, d), F32)
    x_sorted = _dispatch(pad_start, route_p, xnp_, x_sorted)
    x_sorted = _dispatch(pad_start, route_s, xns, x_sorted)
    y_sorted = _experts(block_expert, x_sorted, w_gate.astype(BF16), w_up.astype(BF16),
                        w_down.astype(BF16))
    yp = _combine(pad_start, route_p, hp, slab_p, y_sorted)
    ys = _combine(pad_start, route_s, hs, slab_s, y_sorted)

    h4 = lambda a, n, s: a.reshape(n, n_heads, HEAD_DIM, s).transpose(0, 3, 1, 2)
    new_kv = (h4(k1p, n_p, s_p), h4(v1p, n_p, s_p), h4(k2p, n_p, s_p), h4(v2p, n_p, s_p),
              h4(k1s, 1, t_s).reshape(n_s, s_s, n_heads, HEAD_DIM),
              h4(v1s, 1, t_s).reshape(n_s, s_s, n_heads, HEAD_DIM),
              h4(k2s, 1, t_s).reshape(n_s, s_s, n_heads, HEAD_DIM),
              h4(v2s, 1, t_s).reshape(n_s, s_s, n_heads, HEAD_DIM))
    return yp.reshape(n_p, s_p, d), ys.reshape(n_s, s_s, d), new_kv


def kernel(x_prompt, x_sample, cache_k_sb, cache_v_sb, cache_k_mb, cache_v_mb, page_table, norm_attn, w_in, q_norm_mb, k_norm_mb, out_norm_sb, out_norm_mb, w_out, norm_ffn, w_router_group, b_router_group, w_router_expert, b_router_expert, w_gate, w_up, w_down):
    depth = w_in.shape[0]
    xp, xs = x_prompt, x_sample
    kv = []
    for l in range(depth):
        xp, xs, new_kv = _layer(
            xp, xs, cache_k_sb[l], cache_v_sb[l], cache_k_mb[l], cache_v_mb[l], page_table,
            norm_attn[l], w_in[l], q_norm_mb[l], k_norm_mb[l], out_norm_sb[l], out_norm_mb[l],
            w_out[l], norm_ffn[l], w_router_group[l], b_router_group[l], w_router_expert[l],
            b_router_expert[l], w_gate[l], w_up[l], w_down[l])
        kv.append(new_kv)
    stacked = tuple(jnp.stack([layer_kv[i] for layer_kv in kv]) for i in range(8))
    return (xp, xs) + stacked
```

```python
import functools

import jax
import jax.numpy as jnp
from jax import lax
from jax.experimental import pallas as pl
from jax.experimental.pallas import tpu as pltpu

F32 = jnp.float32
BF16 = jnp.bfloat16
I32 = jnp.int32

HEAD_DIM = 64
PAIR = 128
MOBA_BLOCK = 256
MOBA_TOPK = 3
MOBA_GROUP = 4
N_GROUPS = 4
EXPERTS_PER_GROUP = 8
N_EXPERTS = N_GROUPS * EXPERTS_PER_GROUP
RMS_EPS = 1e-6
SCALE = HEAD_DIM ** -0.5
QB = 128
KB = 128
TOK_TILE = 256
SLOT_BLOCK = 256
ROUTE_LANE0 = N_GROUPS
SB_SKIP = 90.0
NEG = -1e30
VMEM_LIMIT = 56 * 1024 * 1024


def _cparams(sem):
    return pltpu.CompilerParams(dimension_semantics=sem, vmem_limit_bytes=VMEM_LIMIT)


def _split_bf16(x):
    hi = x.astype(BF16)
    lo = (x - hi.astype(F32)).astype(BF16)
    return hi, lo


def _dot(a, b):
    return jnp.dot(a, b, preferred_element_type=F32)


def _dot_nt(a, b):
    return lax.dot_general(a, b, (((1,), (1,)), ((), ())), preferred_element_type=F32)


def _proj_kernel(x_ref, g_ref, w_ref, qn_ref, kn_ref, seg_ref,
                 k1t_ref, v1t_ref, k2t_ref, v2t_ref, q1s_ref, q2s_ref,
                 q1b_ref, k1b_ref, v1b_ref, q2b_ref, k2b_ref, v2b_ref, v2tb_ref, km_ref):
    x = x_ref[...]
    tl = x.shape[0]
    ms = jnp.mean(x * x, axis=-1, keepdims=True)
    xn = x * lax.rsqrt(ms + RMS_EPS) * g_ref[...]
    proj = _dot(xn.astype(BF16), w_ref[...])
    dh = q1s_ref.shape[1]
    seg = seg_ref[...]

    def headnorm(t, g):
        hi, lo = _split_bf16(t * t)
        ssum = _dot(hi, seg) + _dot(lo, seg)
        return t * lax.rsqrt(ssum * (1.0 / HEAD_DIM) + RMS_EPS) * g

    q1 = proj[:, 0 * dh:1 * dh]
    k1 = proj[:, 1 * dh:2 * dh]
    v1 = proj[:, 2 * dh:3 * dh]
    q2 = headnorm(proj[:, 3 * dh:4 * dh], qn_ref[...])
    k2 = headnorm(proj[:, 4 * dh:5 * dh], kn_ref[...])
    v2 = proj[:, 5 * dh:6 * dh]
    v2t = v2.T
    for ref, val in ((k1t_ref, k1.T), (v1t_ref, v1.T), (k2t_ref, k2.T), (v2t_ref, v2t)):
        ref[0] = val
    q1s = q1 * SCALE
    q2s = q2 * SCALE
    q1s_ref[...] = q1s
    q2s_ref[...] = q2s
    for bref, val in ((q1b_ref, q1s), (k1b_ref, k1), (v1b_ref, v1),
                      (q2b_ref, q2s), (k2b_ref, k2), (v2b_ref, v2)):
        bref[...] = val.astype(BF16)
    v2tb_ref[...] = v2t.astype(BF16)
    for c in range(tl // MOBA_BLOCK):
        blk = k2[c * MOBA_BLOCK:(c + 1) * MOBA_BLOCK]
        km_ref[c] = jnp.sum(blk, axis=0, keepdims=True) * (1.0 / MOBA_BLOCK)


def _project(x, n_seq, norm_g, w_in_b, q_norm, k_norm):
    t, d = x.shape
    s = t // n_seq
    dh = w_in_b.shape[1] // 6
    n_heads = dh // HEAD_DIM
    assert s % TOK_TILE == 0 and TOK_TILE % MOBA_BLOCK == 0
    tps = s // TOK_TILE
    r = jnp.arange(dh) // HEAD_DIM
    seg = (r[:, None] == r[None, :]).astype(BF16)
    qn = jnp.tile(q_norm, n_heads)[None]
    kn = jnp.tile(k_norm, n_heads)[None]
    tile = lambda i: (i, 0)
    const = lambda i: (0, 0)
    t_spec = pl.BlockSpec((1, dh, TOK_TILE), lambda i: (i // tps, 0, i % tps))
    f_spec = pl.BlockSpec((TOK_TILE, dh), tile)
    n_km = TOK_TILE // MOBA_BLOCK
    return pl.pallas_call(
        _proj_kernel,
        grid=(t // TOK_TILE,),
        in_specs=[pl.BlockSpec((TOK_TILE, d), tile), pl.BlockSpec((1, d), const),
                  pl.BlockSpec(w_in_b.shape, const), pl.BlockSpec((1, dh), const),
                  pl.BlockSpec((1, dh), const), pl.BlockSpec((dh, dh), const)],
        out_specs=[t_spec] * 4 + [f_spec] * 8
        + [pl.BlockSpec((dh, TOK_TILE), lambda i: (0, i)),
           pl.BlockSpec((n_km, 1, dh), lambda i: (i, 0, 0))],
        out_shape=[jax.ShapeDtypeStruct((n_seq, dh, s), F32)] * 4
        + [jax.ShapeDtypeStruct((t, dh), F32)] * 2
        + [jax.ShapeDtypeStruct((t, dh), BF16)] * 6
        + [jax.ShapeDtypeStruct((dh, t), BF16),
           jax.ShapeDtypeStruct((t // MOBA_BLOCK, 1, dh), F32)],
        compiler_params=_cparams(("parallel",)),
        name="in_proj",
    )(x, norm_g[None], w_in_b, qn, kn, seg)


def _sb_weights(z, carry, cum_u, valid):
    lk = -(jnp.maximum(z, 0.0) + jnp.log1p(jnp.exp(-jnp.abs(z))))
    if valid is not None:
        lk = jnp.where(valid, lk, 0.0)
    hi, lo = _split_bf16(lk)
    cs = _dot(hi, cum_u) + _dot(lo, cum_u)
    a = jnp.exp(z + cs + carry)
    if valid is not None:
        a = jnp.where(valid, a, 0.0)
    return a, cs


def _sb_block(qh, k, v, carry, acc, cum_u, valid):
    a, cs = _sb_weights(_dot_nt(qh, k), carry, cum_u, valid)
    return carry + cs[:, 0:1], acc + _dot(a.astype(BF16), v)


def _cum_matrix(n):
    r = lax.broadcasted_iota(I32, (n, n), 0)
    c = lax.broadcasted_iota(I32, (n, n), 1)
    return (r >= c).astype(BF16)


def _sb_prompt_kernel(q_ref, k_ref, v_ref, o_ref):
    qb = pl.program_id(2)
    q = q_ref[...]
    head0 = lax.broadcasted_iota(I32, (1, PAIR), 1) < HEAD_DIM
    r2 = lax.broadcasted_iota(I32, (2 * KB, 2 * KB), 0)
    c2 = lax.broadcasted_iota(I32, (2 * KB, 2 * KB), 1)
    cum_u = jnp.logical_and(r2 >= c2, (r2 < KB) == (c2 < KB)).astype(BF16)
    r = lax.broadcasted_iota(I32, (QB, 2 * KB), 0)
    c = lax.broadcasted_iota(I32, (QB, 2 * KB), 1)
    left = c < KB
    strictly_before = jnp.where(left, c, c - KB) < r

    def stacked(ref, kb):
        x = ref[pl.ds(pl.multiple_of(kb * KB, KB), KB), :]
        zero = jnp.zeros_like(x)
        return jnp.concatenate([jnp.where(head0, x, zero), jnp.where(head0, zero, x)], axis=0)

    def block(kb, c0, c1, acc, valid):
        z = _dot_nt(q, stacked(k_ref, kb))
        a, cs = _sb_weights(z, jnp.where(left, c0, c1), cum_u, valid)
        return (c0 + cs[:, 0:1], c1 + cs[:, KB:KB + 1],
                acc + _dot(a.astype(BF16), stacked(v_ref, kb)))

    def top(c0, c1):
        return jnp.max(jnp.maximum(c0, c1))

    zero_col = jnp.zeros((QB, 1), F32)
    c0, c1, acc = block(qb, zero_col, zero_col, jnp.zeros((QB, PAIR), F32), strictly_before)

    def cond(st):
        return jnp.logical_and(st[0] >= 0, st[1] > -SB_SKIP)

    def body(st):
        kb, _, c0, c1, acc = st
        c0, c1, acc = block(kb, c0, c1, acc, None)
        return kb - 1, top(c0, c1), c0, c1, acc

    st = lax.while_loop(cond, body, (qb - 1, top(c0, c1), c0, c1, acc))
    o_ref[...] = st[4]


def _sb_prompt(q1b, k1b, v1b, n_seq, seq_len):
    t_all, dh = q1b.shape
    n_pairs = dh // PAIR
    n_qb = seq_len // QB
    return pl.pallas_call(
        _sb_prompt_kernel,
        grid=(n_seq, n_pairs, n_qb),
        in_specs=[pl.BlockSpec((QB, PAIR), lambda n, j, i: (n * n_qb + i, j)),
                  pl.BlockSpec((seq_len, PAIR), lambda n, j, i: (n, j)),
                  pl.BlockSpec((seq_len, PAIR), lambda n, j, i: (n, j))],
        out_specs=pl.BlockSpec((QB, PAIR), lambda n, j, i: (n * n_qb + i, j)),
        out_shape=jax.ShapeDtypeStruct((t_all, dh), F32),
        compiler_params=_cparams(("parallel", "parallel", "arbitrary")),
        name="sb_prompt",
    )(q1b, k1b, v1b)


def _top_blocks(gate, n_past, blk_iota, axis):
    g = jnp.where(blk_iota < n_past, gate, NEG)
    big = jnp.int32(2 ** 30)
    sel = jnp.zeros(gate.shape, F32)
    for r in range(MOBA_TOPK):
        m = jnp.max(g, axis=axis, keepdims=True)
        idx = jnp.min(jnp.where(g == m, blk_iota, big), axis=axis, keepdims=True)
        pick = jnp.logical_and(blk_iota == idx, r < n_past)
        sel = jnp.where(pick, 1.0, sel)
        g = jnp.where(pick, 2.0 * NEG, g)
    return sel


def _softmax_step_t(s_t, v_t, m, l, acc_t):
    m_new = jnp.maximum(m, jnp.max(s_t, axis=0, keepdims=True))
    alpha = jnp.exp(m - m_new)
    p = jnp.exp(s_t - m_new)
    l = alpha * l + jnp.sum(p, axis=0, keepdims=True)
    acc_t = alpha * acc_t + _dot(v_t, p.astype(BF16))
    return m_new, l, acc_t


def _moba_prompt_kernel(slopes_ref, q_ref, k_ref, vt_ref, km_ref, o_ref, sel_ref, cb_ref):
    j = pl.program_id(1)
    qb = pl.program_id(2)
    q = q_ref[...]
    lane = lax.broadcasted_iota(I32, (1, PAIR), 1)
    zero = jnp.zeros_like(q)
    qq = jnp.concatenate([jnp.where(lane < HEAD_DIM, q, zero), jnp.where(lane < HEAD_DIM, zero, q)],
                         axis=0)
    own = (qb * QB) // MOBA_BLOCK
    q_off = qb * QB - own * MOBA_BLOCK
    nb = km_ref.shape[0]
    span = MOBA_GROUP * MOBA_BLOCK
    blk_iota = lax.broadcasted_iota(I32, (nb, 2 * QB), 0)
    km_hi, km_lo = _split_bf16(km_ref[...])
    sel_ref[...] = _top_blocks(_dot_nt(km_hi, qq) + _dot_nt(km_lo, qq), own, blk_iota, 0)
    key = lax.broadcasted_iota(I32, (span, QB), 0).astype(F32)
    slopes = [slopes_ref[2 * j + h] for h in range(2)]
    for h in range(2):
        cb_ref[h] = slopes[h] * key

    def head_cols(x, h):
        return x[:, h * QB:(h + 1) * QB]

    def body(g, st):
        start = pl.multiple_of(g * span, span)
        qk = _dot_nt(k_ref[pl.ds(start, span), :], qq)
        base = (g * span - qb * QB).astype(F32)
        ps, out = [], []
        for h in range(2):
            m, l = st[3 * h], st[3 * h + 1]
            tiles = []
            for b in range(MOBA_GROUP):
                rows = slice(b * MOBA_BLOCK, (b + 1) * MOBA_BLOCK)
                picked = head_cols(sel_ref[pl.ds(g * MOBA_GROUP + b, 1), :], h)
                row = jnp.where(picked > 0.0, slopes[h] * base, NEG)
                tiles.append(head_cols(qk[rows], h) + cb_ref[h, rows, :] + row)
            s_t = jnp.concatenate(tiles, axis=0)
            m_new = jnp.maximum(m, jnp.max(s_t, axis=0, keepdims=True))
            alpha = jnp.exp(m - m_new)
            p = jnp.exp(s_t - m_new)
            ps.append(p.astype(BF16))
            out.append((m_new, alpha * l + jnp.sum(p, axis=0, keepdims=True), alpha))
        pv = _dot(vt_ref[:, pl.ds(start, span)], jnp.concatenate(ps, axis=1))
        new = []
        for h in range(2):
            m_new, l_new, alpha = out[h]
            d_rows = slice(h * HEAD_DIM, (h + 1) * HEAD_DIM)
            new.extend((m_new, l_new, alpha * st[3 * h + 2] + head_cols(pv[d_rows], h)))
        return tuple(new)

    init = (jnp.full((1, QB), NEG, F32), jnp.zeros((1, QB), F32), jnp.zeros((HEAD_DIM, QB), F32)) * 2
    st = lax.fori_loop(0, (own + MOBA_GROUP - 1) // MOBA_GROUP, body, init)

    start = pl.multiple_of(own * MOBA_BLOCK, MOBA_BLOCK)
    k = k_ref[pl.ds(start, MOBA_BLOCK), :]
    vt = vt_ref[:, pl.ds(start, MOBA_BLOCK)]
    qk = _dot_nt(k, qq)
    blk = slice(0, MOBA_BLOCK)
    causal = (lax.broadcasted_iota(I32, (MOBA_BLOCK, QB), 0)
              <= lax.broadcasted_iota(I32, (MOBA_BLOCK, QB), 1) + q_off)
    res = []
    for h in range(2):
        m, l, acc = st[3 * h:3 * h + 3]
        s_t = head_cols(qk, h) + cb_ref[h, blk, :] - slopes[h] * q_off.astype(F32)
        s_t = jnp.where(causal, s_t, NEG)
        m, l, acc = _softmax_step_t(s_t, vt[h * HEAD_DIM:(h + 1) * HEAD_DIM], m, l, acc)
        res.append(acc / l)
    o_ref[...] = jnp.concatenate(res, axis=0).T


def _moba_prompt(slopes, q2b, k2b, v2t, km, n_seq, seq_len):
    t_all, dh = q2b.shape
    n_pairs = dh // PAIR
    n_qb = seq_len // QB
    nb = km.shape[0] // n_seq
    span = MOBA_GROUP * MOBA_BLOCK
    assert seq_len % span == 0 and nb % MOBA_GROUP == 0
    return pl.pallas_call(
        _moba_prompt_kernel,
        grid=(n_seq, n_pairs, n_qb),
        in_specs=[pl.BlockSpec(memory_space=pltpu.SMEM),
                  pl.BlockSpec((QB, PAIR), lambda n, j, i: (n * n_qb + i, j)),
                  pl.BlockSpec((seq_len, PAIR), lambda n, j, i: (n, j)),
                  pl.BlockSpec((PAIR, seq_len), lambda n, j, i: (j, n)),
                  pl.BlockSpec((nb, PAIR), lambda n, j, i: (n, j))],
        out_specs=pl.BlockSpec((QB, PAIR), lambda n, j, i: (n * n_qb + i, j)),
        out_shape=jax.ShapeDtypeStruct((t_all, dh), F32),
        scratch_shapes=[pltpu.VMEM((nb, 2 * QB), F32), pltpu.VMEM((2, span, QB), F32)],
        compiler_params=_cparams(("parallel", "parallel", "arbitrary")),
        name="moba_prompt",
    )(slopes, q2b, k2b, v2t, km)


def _head_rows(q, n_heads):
    s, dh = q.shape
    rows = jnp.concatenate([q] * n_heads, axis=0)
    rr = lax.broadcasted_iota(I32, (n_heads * s, dh), 0) // s
    ll = lax.broadcasted_iota(I32, (n_heads * s, dh), 1) // HEAD_DIM
    keep = rr == ll
    return jnp.where(keep, rows, 0.0), keep


def _merge_head_rows(acc, keep, n_heads):
    s = acc.shape[0] // n_heads
    masked = jnp.where(keep, acc, 0.0)
    out = masked[0:s]
    for h in range(1, n_heads):
        out = out + masked[h * s:(h + 1) * s]
    return out


def _sb_sample_kernel(pt_ref, q_ref, kn_ref, vn_ref, kc_hbm, vc_hbm, o_ref, kbuf, vbuf, sem,
                      *, n_heads, n_pages):
    b = pl.program_id(0)
    s, dh = q_ref.shape

    def slot_of(p):
        return (n_pages - 1 - p) % 2

    def copies(p):
        pg = pt_ref[b, p]
        slot = slot_of(p)
        return (pltpu.make_async_copy(kc_hbm.at[pg], kbuf.at[slot], sem.at[0, slot]),
                pltpu.make_async_copy(vc_hbm.at[pg], vbuf.at[slot], sem.at[1, slot]))

    def start(p):
        for cp in copies(p):
            cp.start()

    def wait(p):
        for cp in copies(p):
            cp.wait()

    start(n_pages - 1)
    start(n_pages - 2)
    qrows, keep = _head_rows(q_ref[...], n_heads)
    qrows = qrows.astype(BF16)
    rows = n_heads * s
    cum_u = _cum_matrix(KB)
    r = lax.broadcasted_iota(I32, (rows, KB), 0) % s
    c = lax.broadcasted_iota(I32, (rows, KB), 1)
    carry, acc = _sb_block(qrows, kn_ref[...], vn_ref[...], jnp.zeros((rows, 1), F32),
                           jnp.zeros((rows, dh), F32), cum_u, c < r)

    def cond(st):
        return jnp.logical_and(st[0] >= 0, st[1] > -SB_SKIP)

    def body(st):
        p, _, carry, acc = st
        wait(p)
        slot = slot_of(p)
        a, cs = _sb_weights(_dot(qrows, kbuf[slot].astype(BF16)), carry, cum_u, None)
        carry = carry + cs[:, 0:1]
        acc = acc + _dot_nt(a.astype(BF16), vbuf[slot].astype(BF16))

        @pl.when(p >= 2)
        def _():
            start(p - 2)

        return p - 1, jnp.max(carry), carry, acc

    st = lax.while_loop(cond, body, (jnp.int32(n_pages - 1), jnp.max(carry), carry, acc))
    p_exit = st[0]

    @pl.when(p_exit >= 0)
    def _():
        wait(p_exit)

    @pl.when(p_exit >= 1)
    def _():
        wait(p_exit - 1)

    o_ref[...] = _merge_head_rows(st[3], keep, n_heads)


def _sb_sample(page_table, q1s, k_new, v_new, cache_kt, cache_vt):
    n_b, n_pages = page_table.shape
    dh = q1s.shape[1]
    s = q1s.shape[0] // n_b
    page = cache_kt.shape[2]
    assert page == KB and n_pages >= 2
    n_heads = dh // HEAD_DIM
    per_b = lambda b, pt: (b, 0)
    grid_spec = pltpu.PrefetchScalarGridSpec(
        num_scalar_prefetch=1,
        grid=(n_b,),
        in_specs=[pl.BlockSpec((s, dh), per_b), pl.BlockSpec((KB, dh), per_b),
                  pl.BlockSpec((KB, dh), per_b), pl.BlockSpec(memory_space=pl.ANY),
                  pl.BlockSpec(memory_space=pl.ANY)],
        out_specs=pl.BlockSpec((s, dh), per_b),
        scratch_shapes=[pltpu.VMEM((2, dh, page), F32), pltpu.VMEM((2, dh, page), F32),
                        pltpu.SemaphoreType.DMA((2, 2))],
    )
    return pl.pallas_call(
        functools.partial(_sb_sample_kernel, n_heads=n_heads, n_pages=n_pages),
        grid_spec=grid_spec,
        out_shape=jax.ShapeDtypeStruct(q1s.shape, F32),
        compiler_params=_cparams(("arbitrary",)),
        name="sb_sample",
    )(page_table, q1s, k_new, v_new, cache_kt, cache_vt)


def _blocks_per_step(n_blk):
    return max(g for g in (4, 2, 1) if n_blk % g == 0)


def _moba_pick_kernel(pt_ref, q_ref, *refs, n_heads, n_pg):
    del pt_ref
    k_refs, (sel_ref, gate_ref, qt_ref) = refs[:n_pg], refs[n_pg:]
    step = pl.program_id(1)
    n_steps = pl.num_programs(1)
    per_step = n_pg // 2
    rows_p = gate_ref.shape[1]

    @pl.when(step == 0)
    def _():
        qrows, _ = _head_rows(q_ref[...], n_heads)
        pad = jnp.zeros((rows_p - qrows.shape[0], qrows.shape[1]), F32)
        qt_ref[...] = jnp.concatenate([qrows, pad], axis=0).T
        gate_ref[...] = jnp.zeros(gate_ref.shape, F32)

    qt = qt_ref[...]
    for bb in range(per_step):
        ksum = (jnp.sum(k_refs[2 * bb][0], axis=1, keepdims=True)
                + jnp.sum(k_refs[2 * bb + 1][0], axis=1, keepdims=True))
        gate_ref[pl.ds(step * per_step + bb, 1), :] = jnp.sum(
            qt * (ksum * (1.0 / MOBA_BLOCK)), axis=0, keepdims=True)

    @pl.when(step == n_steps - 1)
    def _():
        blk_iota = lax.broadcasted_iota(I32, gate_ref.shape, 0)
        sel_t = _top_blocks(gate_ref[...], n_steps * per_step, blk_iota, 0)
        sel_ref[0] = sel_t.T[0:sel_ref.shape[1]]


def _moba_pick(page_table, q2s, cache_kt, nbp):
    n_b, n_pages = page_table.shape
    dh = q2s.shape[1]
    s = q2s.shape[0] // n_b
    page = cache_kt.shape[2]
    n_heads = dh // HEAD_DIM
    rows = n_heads * s
    n_blk = n_pages // 2
    assert 2 * page == MOBA_BLOCK and nbp == PAIR and rows <= PAIR and n_blk <= nbp
    per_step = _blocks_per_step(n_blk)
    n_pg = 2 * per_step

    def pg(i):
        return lambda b, m, pt: (pt[b, n_pg * m + i], 0, 0)

    grid_spec = pltpu.PrefetchScalarGridSpec(
        num_scalar_prefetch=1,
        grid=(n_b, n_blk // per_step),
        in_specs=[pl.BlockSpec((s, dh), lambda b, m, pt: (b, 0))]
        + [pl.BlockSpec((1, dh, page), pg(i)) for i in range(n_pg)],
        out_specs=pl.BlockSpec((1, rows, nbp), lambda b, m, pt: (b, 0, 0)),
        scratch_shapes=[pltpu.VMEM((nbp, PAIR), F32), pltpu.VMEM((dh, PAIR), F32)],
    )
    return pl.pallas_call(
        functools.partial(_moba_pick_kernel, n_heads=n_heads, n_pg=n_pg),
        grid_spec=grid_spec,
        out_shape=jax.ShapeDtypeStruct((n_b, rows, nbp), F32),
        compiler_params=_cparams(("parallel", "arbitrary")),
        name="moba_pick",
    )(page_table, q2s, *([cache_kt] * n_pg))


def _moba_sample_kernel(pt_ref, q_ref, slope_ref, sel_ref, kn_ref, vn_ref, *refs,
                        n_heads, past_len, n_pg):
    del pt_ref
    k_refs, v_refs = refs[:n_pg], refs[n_pg:2 * n_pg]
    o_ref, m_ref, l_ref, acc_ref = refs[2 * n_pg:]
    step = pl.program_id(1)
    n_steps = pl.num_programs(1) - 1
    per_step = n_pg // 2
    s = q_ref.shape[0]
    rows = n_heads * s
    qrows, keep = _head_rows(q_ref[...], n_heads)
    qrows = qrows.astype(BF16)
    nbp = sel_ref.shape[2]
    page = k_refs[0].shape[2]
    span = n_pg * page
    blk_iota = lax.broadcasted_iota(I32, (rows, nbp), 1)
    slope = slope_ref[:, 0:1]

    @pl.when(step == 0)
    def _():
        m_ref[...] = jnp.full(m_ref.shape, NEG, F32)
        l_ref[...] = jnp.zeros(l_ref.shape, F32)
        acc_ref[...] = jnp.zeros(acc_ref.shape, F32)

    def update(qk, ok, rel, pv):
        sc = jnp.where(ok, qk - slope * rel, NEG)
        m = m_ref[:, 0:1]
        m_new = jnp.maximum(m, jnp.max(sc, axis=-1, keepdims=True))
        alpha = jnp.exp(m - m_new)
        p = jnp.where(ok, jnp.exp(sc - m_new), 0.0)
        l = alpha * l_ref[:, 0:1] + jnp.sum(p, axis=-1, keepdims=True)
        acc = alpha * acc_ref[...] + pv(p.astype(BF16))
        m_ref[...] = jnp.broadcast_to(m_new, m_ref.shape)
        l_ref[...] = jnp.broadcast_to(l, l_ref.shape)
        acc_ref[...] = acc
        return l, acc

    def rel_pos(width):
        r = lax.broadcasted_iota(I32, (rows, width), 0) % s
        c = lax.broadcasted_iota(I32, (rows, width), 1)
        return r - c

    @pl.when(step < n_steps)
    def _():
        qk = jnp.concatenate([_dot(qrows, kr[0].astype(BF16)) for kr in k_refs], axis=1)
        sel = sel_ref[0]
        oks = []
        for bb in range(per_step):
            picked = jnp.max(jnp.where(blk_iota == step * per_step + bb, sel, 0.0),
                             axis=-1, keepdims=True)
            oks.append(jnp.broadcast_to(picked, (rows, MOBA_BLOCK)) > 0.0)
        dist0 = (past_len - step * span).astype(F32)

        def pv(p):
            out = _dot_nt(p[:, 0:page], v_refs[0][0].astype(BF16))
            for i in range(1, n_pg):
                out = out + _dot_nt(p[:, i * page:(i + 1) * page], v_refs[i][0].astype(BF16))
            return out

        update(qk, jnp.concatenate(oks, axis=1), rel_pos(span).astype(F32) + dist0, pv)

    @pl.when(step == n_steps)
    def _():
        rel = rel_pos(MOBA_BLOCK)
        l, acc = update(_dot_nt(qrows, kn_ref[...]), rel >= 0, rel.astype(F32),
                        lambda p: _dot(p, vn_ref[...]))
        o_ref[...] = _merge_head_rows(acc / l, keep, n_heads)


def _moba_sample(page_table, q2s, slope_rows, sel, k_new, v_new, cache_kt, cache_vt):
    n_b, n_pages = page_table.shape
    dh = q2s.shape[1]
    s = q2s.shape[0] // n_b
    page = cache_kt.shape[2]
    n_blk = n_pages // 2
    n_heads = dh // HEAD_DIM
    rows = n_heads * s
    nbp = sel.shape[2]
    per_step = _blocks_per_step(n_blk)
    n_pg = 2 * per_step
    n_steps = n_blk // per_step

    def pg(i):
        return lambda b, m, pt: (pt[b, n_pg * jnp.minimum(m, n_steps - 1) + i], 0, 0)

    per_b = lambda b, m, pt: (b, 0)
    page_specs = [pl.BlockSpec((1, dh, page), pg(i)) for i in range(n_pg)]
    grid_spec = pltpu.PrefetchScalarGridSpec(
        num_scalar_prefetch=1,
        grid=(n_b, n_steps + 1),
        in_specs=[pl.BlockSpec((s, dh), per_b),
                  pl.BlockSpec((rows, PAIR), lambda b, m, pt: (0, 0)),
                  pl.BlockSpec((1, rows, nbp), lambda b, m, pt: (b, 0, 0)),
                  pl.BlockSpec((MOBA_BLOCK, dh), per_b),
                  pl.BlockSpec((MOBA_BLOCK, dh), per_b)] + page_specs + page_specs,
        out_specs=pl.BlockSpec((s, dh), per_b),
        scratch_shapes=[pltpu.VMEM((rows, PAIR), F32), pltpu.VMEM((rows, PAIR), F32),
                        pltpu.VMEM((rows, dh), F32)],
    )
    return pl.pallas_call(
        functools.partial(_moba_sample_kernel, n_heads=n_heads, past_len=n_pages * page, n_pg=n_pg),
        grid_spec=grid_spec,
        out_shape=jax.ShapeDtypeStruct(q2s.shape, F32),
        compiler_params=_cparams(("parallel", "arbitrary")),
        name="moba_sample",
    )(page_table, q2s, slope_rows, sel, k_new, v_new, *([cache_kt] * n_pg), *([cache_vt] * n_pg))


def _mix_route_kernel(cnt_in_ref, o1_ref, o2_ref, x_ref, g1_ref, g2_ref, wo_ref, gf_ref,
                      wr_ref, br_ref, h_ref, xn_ref, slab_ref, cnt_ref, run_ref):
    i = pl.program_id(0)

    @pl.when(i == 0)
    def _():
        run_ref[...] = cnt_in_ref[...]

    def rms(t, g):
        return t * lax.rsqrt(jnp.mean(t * t, axis=-1, keepdims=True) + RMS_EPS) * g

    n1 = rms(o1_ref[...], g1_ref[...]).astype(BF16)
    n2 = rms(o2_ref[...], g2_ref[...]).astype(BF16)
    dh = n1.shape[1]
    h = x_ref[...] + _dot(n1, wo_ref[0:dh, :]) + _dot(n2, wo_ref[dh:2 * dh, :])
    h_ref[...] = h
    xn = rms(h, gf_ref[...])
    xn_ref[...] = xn

    x_hi, x_lo = _split_bf16(xn)
    w_hi, w_lo = _split_bf16(wr_ref[...])
    logits = _dot(x_hi, w_hi) + _dot(x_hi, w_lo) + _dot(x_lo, w_hi) + br_ref[...]
    tl, nl = logits.shape
    lane = lax.broadcasted_iota(I32, (tl, nl), 1)
    big = jnp.int32(2 ** 30)

    def first_max(vals):
        m = jnp.max(vals, axis=-1, keepdims=True)
        idx = jnp.min(jnp.where(vals == m, lane, big), axis=-1, keepdims=True)
        return m, idx

    grp_logit = jnp.where(lane < N_GROUPS, logits, NEG)
    gm, grp = first_max(grp_logit)
    denom = jnp.sum(jnp.where(lane < N_GROUPS, jnp.exp(grp_logit - gm), 0.0), axis=-1, keepdims=True)
    p_top = 1.0 / denom
    e_lo = ROUTE_LANE0 + grp * EXPERTS_PER_GROUP
    in_grp = jnp.logical_and(lane >= e_lo, lane < e_lo + EXPERTS_PER_GROUP)
    e_logit = jnp.where(in_grp, logits, NEG)
    m1, i1 = first_max(e_logit)
    m2, i2 = first_max(jnp.where(lane == i1, 2.0 * NEG, e_logit))
    e2 = jnp.exp(m2 - m1)
    gate1 = p_top / (1.0 + e2)
    gate2 = p_top * e2 / (1.0 + e2)

    hot1 = lane == i1
    hot2 = lane == i2
    both = jnp.logical_or(hot1, hot2)
    rr = lax.broadcasted_iota(I32, (tl, tl), 0)
    cc = lax.broadcasted_iota(I32, (tl, tl), 1)
    earlier = (cc < rr).astype(BF16)
    before = _dot(earlier, jnp.where(both, 1.0, 0.0).astype(BF16)) + run_ref[...]
    rank1 = jnp.sum(jnp.where(hot1, before, 0.0), axis=-1, keepdims=True)
    rank2 = jnp.sum(jnp.where(hot2, before, 0.0), axis=-1, keepdims=True)
    run_ref[...] = run_ref[...] + jnp.sum(jnp.where(both, 1.0, 0.0), axis=0, keepdims=True)
    cnt_ref[...] = run_ref[...]

    eid1 = (i1 - ROUTE_LANE0).astype(F32)
    eid2 = (i2 - ROUTE_LANE0).astype(F32)
    cols = (eid1, eid2, rank1, rank2, gate1, gate2)
    slab = jnp.zeros((tl, nl), F32)
    for n, col in enumerate(cols):
        slab = jnp.where(lane == n, col, slab)
    slab_ref[...] = slab


def _mix_route(cnt_in, o1, o2, x, g1, g2, w_out_b, g_ffn, w_route, b_route):
    t, d = x.shape
    dh = o1.shape[1]
    assert t % TOK_TILE == 0
    tile = lambda i: (i, 0)
    const = lambda i: (0, 0)
    return pl.pallas_call(
        _mix_route_kernel,
        grid=(t // TOK_TILE,),
        in_specs=[pl.BlockSpec((1, PAIR), const),
                  pl.BlockSpec((TOK_TILE, dh), tile),
                  pl.BlockSpec((TOK_TILE, dh), tile),
                  pl.BlockSpec((TOK_TILE, d), tile),
                  pl.BlockSpec((1, dh), const), pl.BlockSpec((1, dh), const),
                  pl.BlockSpec((d, d), const), pl.BlockSpec((1, d), const),
                  pl.BlockSpec((d, PAIR), const), pl.BlockSpec((1, PAIR), const)],
        out_specs=[pl.BlockSpec((TOK_TILE, d), tile), pl.BlockSpec((TOK_TILE, d), tile),
                   pl.BlockSpec((TOK_TILE, PAIR), tile), pl.BlockSpec((1, PAIR), const)],
        out_shape=[jax.ShapeDtypeStruct((t, d), F32), jax.ShapeDtypeStruct((t, d), F32),
                   jax.ShapeDtypeStruct((t, PAIR), F32), jax.ShapeDtypeStruct((1, PAIR), F32)],
        scratch_shapes=[pltpu.VMEM((1, PAIR), F32)],
        compiler_params=_cparams(("arbitrary",)),
        name="mix_route",
    )(cnt_in, o1, o2, x, g1[None], g2[None], w_out_b, g_ffn[None], w_route, b_route)


def _row_copy(src, src_row, dst, dst_row, sem):
    return pltpu.make_async_copy(src.at[pl.ds(src_row, 1)], dst.at[pl.ds(dst_row, 1)], sem)


def _load_route(route_hbm, route_smem, sem):
    cp = pltpu.make_async_copy(route_hbm.at[pl.program_id(0)], route_smem, sem)
    cp.start()
    cp.wait()


def _dispatch_kernel(start_ref, route_hbm, xn_ref, xs_in, xs_out, route_smem, rsem, sem):
    del xs_in
    _load_route(route_hbm, route_smem, rsem)
    tl = xn_ref.shape[0]

    def slot(i, k):
        return start_ref[route_smem[4 * i + k]] + route_smem[4 * i + 2 + k]

    def issue(i, carry):
        _row_copy(xn_ref, i, xs_out, slot(i, 0), sem).start(priority=0)
        _row_copy(xn_ref, i, xs_out, slot(i, 1), sem).start(priority=1)
        return carry

    def drain(i, carry):
        _row_copy(xn_ref, 0, xs_out, 0, sem).wait()
        _row_copy(xn_ref, 0, xs_out, 0, sem).wait()
        return carry

    lax.fori_loop(0, tl, issue, 0)
    lax.fori_loop(0, tl, drain, 0)


def _dispatch(pad_start, route_i, xn, xs):
    t, d = xn.shape
    n_tiles = t // TOK_TILE
    grid_spec = pltpu.PrefetchScalarGridSpec(
        num_scalar_prefetch=1,
        grid=(n_tiles,),
        in_specs=[pl.BlockSpec(memory_space=pl.ANY),
                  pl.BlockSpec((TOK_TILE, d), lambda i, st: (i, 0)),
                  pl.BlockSpec(memory_space=pl.ANY)],
        out_specs=pl.BlockSpec(memory_space=pl.ANY),
        scratch_shapes=[pltpu.SMEM((4 * TOK_TILE,), I32), pltpu.SemaphoreType.DMA,
                        pltpu.SemaphoreType.DMA],
    )
    return pl.pallas_call(
        _dispatch_kernel,
        grid_spec=grid_spec,
        out_shape=jax.ShapeDtypeStruct(xs.shape, xs.dtype),
        input_output_aliases={3: 0},
        compiler_params=_cparams(("arbitrary",)),
        name="dispatch",
    )(pad_start, route_i, xn, xs)


def _expert_kernel(be_ref, x_ref, wg_ref, wu_ref, wd_ref, y_ref):
    del be_ref
    x = x_ref[...].astype(BF16)
    g = _dot(x, wg_ref[0])
    u = _dot(x, wu_ref[0])
    hid = g * (1.0 / (1.0 + jnp.exp(-g))) * u
    y_ref[...] = _dot(hid.astype(BF16), wd_ref[0])


def _experts(block_expert, xs, wg_b, wu_b, wd_b):
    n_slots, d = xs.shape
    de = wg_b.shape[2]
    n_blocks = n_slots // SLOT_BLOCK
    grid_spec = pltpu.PrefetchScalarGridSpec(
        num_scalar_prefetch=1,
        grid=(n_blocks,),
        in_specs=[pl.BlockSpec((SLOT_BLOCK, d), lambda b, be: (b, 0)),
                  pl.BlockSpec((1, d, de), lambda b, be: (be[b], 0, 0)),
                  pl.BlockSpec((1, d, de), lambda b, be: (be[b], 0, 0)),
                  pl.BlockSpec((1, de, d), lambda b, be: (be[b], 0, 0))],
        out_specs=pl.BlockSpec((SLOT_BLOCK, d), lambda b, be: (b, 0)),
    )
    return pl.pallas_call(
        _expert_kernel,
        grid_spec=grid_spec,
        out_shape=jax.ShapeDtypeStruct((n_slots, d), F32),
        compiler_params=_cparams(("arbitrary",)),
        name="experts",
    )(block_expert, xs, wg_b, wu_b, wd_b)


def _combine_kernel(start_ref, route_hbm, h_ref, slab_ref, ys_hbm, out_ref,
                    route_smem, y_buf, rsem, sem):
    _load_route(route_hbm, route_smem, rsem)
    tl = h_ref.shape[0]

    def slot(i, k):
        return start_ref[route_smem[4 * i + k]] + route_smem[4 * i + 2 + k]

    def issue(i, carry):
        _row_copy(ys_hbm, slot(i, 0), y_buf.at[0], i, sem).start(priority=0)
        _row_copy(ys_hbm, slot(i, 1), y_buf.at[1], i, sem).start(priority=1)
        return carry

    def drain(i, carry):
        _row_copy(ys_hbm, 0, y_buf.at[0], 0, sem).wait()
        _row_copy(ys_hbm, 0, y_buf.at[1], 0, sem).wait()
        return carry

    lax.fori_loop(0, tl, issue, 0)
    lax.fori_loop(0, tl, drain, 0)
    slab = slab_ref[...]
    out_ref[...] = h_ref[...] + slab[:, 4:5] * y_buf[0] + slab[:, 5:6] * y_buf[1]


def _combine(pad_start, route_i, h, slab, ys):
    t, d = h.shape
    n_tiles = t // TOK_TILE
    grid_spec = pltpu.PrefetchScalarGridSpec(
        num_scalar_prefetch=1,
        grid=(n_tiles,),
        in_specs=[pl.BlockSpec(memory_space=pl.ANY),
                  pl.BlockSpec((TOK_TILE, d), lambda i, st: (i, 0)),
                  pl.BlockSpec((TOK_TILE, PAIR), lambda i, st: (i, 0)),
                  pl.BlockSpec(memory_space=pl.ANY)],
        out_specs=pl.BlockSpec((TOK_TILE, d), lambda i, st: (i, 0)),
        scratch_shapes=[pltpu.SMEM((4 * TOK_TILE,), I32), pltpu.VMEM((2, TOK_TILE, d), F32),
                        pltpu.SemaphoreType.DMA, pltpu.SemaphoreType.DMA],
    )
    return pl.pallas_call(
        _combine_kernel,
        grid_spec=grid_spec,
        out_shape=jax.ShapeDtypeStruct((t, d), F32),
        compiler_params=_cparams(("arbitrary",)),
        name="combine",
    )(pad_start, route_i, h, slab, ys)


def _route_ints(slab):
    t = slab.shape[0]
    return slab[:, 0:4].astype(I32).reshape(t // TOK_TILE, 4 * TOK_TILE)


def _pad_rows(x, n_b, rows):
    s = x.shape[0] // n_b
    x = x.reshape(n_b, s, x.shape[1])
    return jnp.pad(x, ((0, 0), (0, rows - s), (0, 0))).reshape(n_b * rows, -1)


def _layer(xp, xs, cache_k_sb, cache_v_sb, cache_k_mb, cache_v_mb, page_table,
           norm_attn, w_in, q_norm_mb, k_norm_mb, out_norm_sb, out_norm_mb, w_out,
           norm_ffn, w_rg, b_rg, w_re, b_re, w_gate, w_up, w_down):
    n_p, s_p, d = xp.shape
    n_s, s_s, _ = xs.shape
    t_p, t_s = n_p * s_p, n_s * s_s
    t_all = t_p + t_s
    dh = w_in.shape[1] // 6
    n_heads = dh // HEAD_DIM
    assert s_p % MOBA_BLOCK == 0 and dh % PAIR == 0
    xp2, xs2 = xp.reshape(t_p, d), xs.reshape(t_s, d)
    w_in_b = w_in.astype(BF16)
    slopes = 2.0 ** (-8.0 * jnp.arange(1, n_heads + 1, dtype=F32) / n_heads)

    pp = _project(xp2, n_p, norm_attn, w_in_b, q_norm_mb, k_norm_mb)
    ps = _project(xs2, 1, norm_attn, w_in_b, q_norm_mb, k_norm_mb)
    (k1p, v1p, k2p, v2p, _, _, q1pb, k1pb, v1pb, q2pb, k2pb, _, v2ptb, kmp) = pp
    (k1s, v1s, k2s, v2s, q1s, q2s, _, k1sb, v1sb, _, k2sb, v2sb, _, _) = ps

    nb = s_p // MOBA_BLOCK
    nb8 = -(-nb // 8) * 8
    kmp = jnp.pad(kmp.reshape(n_p, nb, dh), ((0, 0), (0, nb8 - nb), (0, 0))).reshape(n_p * nb8, dh)
    o_sb_p = _sb_prompt(q1pb, k1pb, v1pb, n_p, s_p)
    o_mb_p = _moba_prompt(slopes, q2pb, k2pb, v2ptb, kmp, n_p, s_p)

    pool, page = cache_k_sb.shape[0], cache_k_sb.shape[1]
    as_t = lambda c: c.transpose(0, 2, 3, 1).reshape(pool, dh, page)
    o_sb_s = _sb_sample(page_table, q1s, _pad_rows(k1sb, n_s, KB), _pad_rows(v1sb, n_s, KB),
                        as_t(cache_k_sb), as_t(cache_v_sb))
    c_k_mb, c_v_mb = as_t(cache_k_mb), as_t(cache_v_mb)
    n_blk_s = page_table.shape[1] * page // MOBA_BLOCK
    nbp_s = -(-n_blk_s // PAIR) * PAIR
    sel_s = _moba_pick(page_table, q2s, c_k_mb, nbp_s)
    slope_rows = jnp.broadcast_to(jnp.repeat(slopes, s_s)[:, None], (n_heads * s_s, PAIR))
    o_mb_s = _moba_sample(page_table, q2s, slope_rows, sel_s, _pad_rows(k2sb, n_s, MOBA_BLOCK),
                          _pad_rows(v2sb, n_s, MOBA_BLOCK), c_k_mb, c_v_mb)

    w_out_b = w_out.astype(BF16)
    w_route = jnp.pad(jnp.concatenate([w_rg, w_re], axis=1), ((0, 0), (0, PAIR - N_GROUPS - N_EXPERTS)))
    b_route = jnp.pad(jnp.concatenate([b_rg, b_re]), (0, PAIR - N_GROUPS - N_EXPERTS))[None]
    cnt0 = jnp.zeros((1, PAIR), F32)
    hp, xnp_, slab_p, cnt1 = _mix_route(cnt0, o_sb_p, o_mb_p, xp2, out_norm_sb, out_norm_mb,
                                        w_out_b, norm_ffn, w_route, b_route)
    hs, xns, slab_s, cnt2 = _mix_route(cnt1, o_sb_s, o_mb_s, xs2, out_norm_sb, out_norm_mb,
                                       w_out_b, norm_ffn, w_route, b_route)

    counts = cnt2[0, ROUTE_LANE0:ROUTE_LANE0 + N_EXPERTS].astype(I32)
    padded = (counts + SLOT_BLOCK - 1) // SLOT_BLOCK * SLOT_BLOCK
    pad_end = jnp.cumsum(padded)
    pad_start = (pad_end - padded).astype(I32)
    n_blocks = -(-2 * t_all // SLOT_BLOCK) + N_EXPERTS
    block_row0 = jnp.arange(n_blocks, dtype=I32) * SLOT_BLOCK
    block_expert = jnp.minimum(
        jnp.sum((pad_end[None, :] <= block_row0[:, None]).astype(I32), axis=1), N_EXPERTS - 1)

    route_p, route_s = _route_ints(slab_p), _route_ints(slab_s)
    x_sorted = jnp.zeros((n_blocks * SLOT_BLOCK, d), F32)
    x_sorted = _dispatch(pad_start, route_p, xnp_, x_sorted)
    x_sorted = _dispatch(pad_start, route_s, xns, x_sorted)
    y_sorted = _experts(block_expert, x_sorted, w_gate.astype(BF16), w_up.astype(BF16),
                        w_down.astype(BF16))
    yp = _combine(pad_start, route_p, hp, slab_p, y_sorted)
    ys = _combine(pad_start, route_s, hs, slab_s, y_sorted)

    h4 = lambda a, n, s: a.reshape(n, n_heads, HEAD_DIM, s).transpose(0, 3, 1, 2)
    new_kv = (h4(k1p, n_p, s_p), h4(v1p, n_p, s_p), h4(k2p, n_p, s_p), h4(v2p, n_p, s_p),
              h4(k1s, 1, t_s).reshape(n_s, s_s, n_heads, HEAD_DIM),
              h4(v1s, 1, t_s).reshape(n_s, s_s, n_heads, HEAD_DIM),
              h4(k2s, 1, t_s).reshape(n_s, s_s, n_heads, HEAD_DIM),
              h4(v2s, 1, t_s).reshape(n_s, s_s, n_heads, HEAD_DIM))
    return yp.reshape(n_p, s_p, d), ys.reshape(n_s, s_s, d), new_kv


def kernel(x_prompt, x_sample, cache_k_sb, cache_v_sb, cache_k_mb, cache_v_mb, page_table, norm_attn, w_in, q_norm_mb, k_norm_mb, out_norm_sb, out_norm_mb, w_out, norm_ffn, w_router_group, b_router_group, w_router_expert, b_router_expert, w_gate, w_up, w_down):
    depth = w_in.shape[0]
    xp, xs = x_prompt, x_sample
    kv = []
    for l in range(depth):
        xp, xs, new_kv = _layer(
            xp, xs, cache_k_sb[l], cache_v_sb[l], cache_k_mb[l], cache_v_mb[l], page_table,
            norm_attn[l], w_in[l], q_norm_mb[l], k_norm_mb[l], out_norm_sb[l], out_norm_mb[l],
            w_out[l], norm_ffn[l], w_router_group[l], b_router_group[l], w_router_expert[l],
            b_router_expert[l], w_gate[l], w_up[l], w_down[l])
        kv.append(new_kv)
    stacked = tuple(jnp.stack([layer_kv[i] for layer_kv in kv]) for i in range(8))
    return (xp, xs) + stacked
```

```python
import functools
import math

import jax
import jax.numpy as jnp
from jax import lax
from jax.experimental import pallas as pl
from jax.experimental.pallas import tpu as pltpu

F32 = jnp.float32
BF16 = jnp.bfloat16
I32 = jnp.int32

HEAD_DIM = 64
PAIR = 128
MOBA_BLOCK = 256
MOBA_TOPK = 3
MOBA_GROUP = 4
MOBA_QB = 256
N_GROUPS = 4
EXPERTS_PER_GROUP = 8
N_EXPERTS = N_GROUPS * EXPERTS_PER_GROUP
RMS_EPS = 1e-6
SCALE = HEAD_DIM ** -0.5
QB = 128
KB = 128
TOK_TILE = 512
SLOT_BLOCK = 256
ROUTE_LANE0 = N_GROUPS
SB_SKIP = 90.0
NEG = -1e30
VMEM_LIMIT = 56 * 1024 * 1024


def _cparams(sem):
    return pltpu.CompilerParams(dimension_semantics=sem, vmem_limit_bytes=VMEM_LIMIT)


def _tok_tile(t):
    return math.gcd(t, TOK_TILE)


def _split_bf16(x):
    hi = x.astype(BF16)
    lo = (x - hi.astype(F32)).astype(BF16)
    return hi, lo


def _dot(a, b):
    return jnp.dot(a, b, preferred_element_type=F32)


def _dot_nt(a, b):
    return lax.dot_general(a, b, (((1,), (1,)), ((), ())), preferred_element_type=F32)


def _proj_kernel(x_ref, g_ref, w_ref, qn_ref, kn_ref, seg_ref,
                 k1t_ref, v1t_ref, k2t_ref, v2t_ref, q1s_ref, q2s_ref,
                 q1b_ref, k1b_ref, v1b_ref, q2b_ref, k2b_ref, v2b_ref, v2tb_ref, km_ref):
    x = x_ref[...]
    tl = x.shape[0]
    ms = jnp.mean(x * x, axis=-1, keepdims=True)
    xn = x * lax.rsqrt(ms + RMS_EPS) * g_ref[...]
    proj = _dot(xn.astype(BF16), w_ref[...])
    dh = q1s_ref.shape[1]
    seg = seg_ref[...]

    def headnorm(t, g):
        hi, lo = _split_bf16(t * t)
        ssum = _dot(hi, seg) + _dot(lo, seg)
        return t * lax.rsqrt(ssum * (1.0 / HEAD_DIM) + RMS_EPS) * g

    q1 = proj[:, 0 * dh:1 * dh]
    k1 = proj[:, 1 * dh:2 * dh]
    v1 = proj[:, 2 * dh:3 * dh]
    q2 = headnorm(proj[:, 3 * dh:4 * dh], qn_ref[...])
    k2 = headnorm(proj[:, 4 * dh:5 * dh], kn_ref[...])
    v2 = proj[:, 5 * dh:6 * dh]
    v2t = v2.T
    for ref, val in ((k1t_ref, k1.T), (v1t_ref, v1.T), (k2t_ref, k2.T), (v2t_ref, v2t)):
        ref[0] = val
    q1s = q1 * SCALE
    q2s = q2 * SCALE
    q1s_ref[...] = q1s
    q2s_ref[...] = q2s
    for bref, val in ((q1b_ref, q1s), (k1b_ref, k1), (v1b_ref, v1),
                      (q2b_ref, q2s), (k2b_ref, k2), (v2b_ref, v2)):
        bref[...] = val.astype(BF16)
    v2tb_ref[...] = v2t.astype(BF16)
    for c in range(tl // MOBA_BLOCK):
        blk = k2[c * MOBA_BLOCK:(c + 1) * MOBA_BLOCK]
        km_ref[c] = jnp.sum(blk, axis=0, keepdims=True) * (1.0 / MOBA_BLOCK)


def _project(x, n_seq, norm_g, w_in_b, q_norm, k_norm):
    t, d = x.shape
    s = t // n_seq
    dh = w_in_b.shape[1] // 6
    n_heads = dh // HEAD_DIM
    tt = _tok_tile(s)
    assert tt % MOBA_BLOCK == 0
    tps = s // tt
    r = jnp.arange(dh) // HEAD_DIM
    seg = (r[:, None] == r[None, :]).astype(BF16)
    qn = jnp.tile(q_norm, n_heads)[None]
    kn = jnp.tile(k_norm, n_heads)[None]
    tile = lambda i: (i, 0)
    const = lambda i: (0, 0)
    t_spec = pl.BlockSpec((1, dh, tt), lambda i: (i // tps, 0, i % tps))
    f_spec = pl.BlockSpec((tt, dh), tile)
    n_km = tt // MOBA_BLOCK
    return pl.pallas_call(
        _proj_kernel,
        grid=(t // tt,),
        in_specs=[pl.BlockSpec((tt, d), tile), pl.BlockSpec((1, d), const),
                  pl.BlockSpec(w_in_b.shape, const), pl.BlockSpec((1, dh), const),
                  pl.BlockSpec((1, dh), const), pl.BlockSpec((dh, dh), const)],
        out_specs=[t_spec] * 4 + [f_spec] * 8
        + [pl.BlockSpec((dh, tt), lambda i: (0, i)),
           pl.BlockSpec((n_km, 1, dh), lambda i: (i, 0, 0))],
        out_shape=[jax.ShapeDtypeStruct((n_seq, dh, s), F32)] * 4
        + [jax.ShapeDtypeStruct((t, dh), F32)] * 2
        + [jax.ShapeDtypeStruct((t, dh), BF16)] * 6
        + [jax.ShapeDtypeStruct((dh, t), BF16),
           jax.ShapeDtypeStruct((t // MOBA_BLOCK, 1, dh), F32)],
        compiler_params=_cparams(("parallel",)),
        name="in_proj",
    )(x, norm_g[None], w_in_b, qn, kn, seg)


def _sb_weights(z, carry, cum_u, valid):
    lk = -(jnp.maximum(z, 0.0) + jnp.log1p(jnp.exp(-jnp.abs(z))))
    if valid is not None:
        lk = jnp.where(valid, lk, 0.0)
    hi, lo = _split_bf16(lk)
    cs = _dot(hi, cum_u) + _dot(lo, cum_u)
    a = jnp.exp(z + cs + carry)
    if valid is not None:
        a = jnp.where(valid, a, 0.0)
    return a, cs


def _sb_block(qh, k, v, carry, acc, cum_u, valid):
    a, cs = _sb_weights(_dot_nt(qh, k), carry, cum_u, valid)
    return carry + cs[:, 0:1], acc + _dot(a.astype(BF16), v)


def _cum_matrix(n):
    r = lax.broadcasted_iota(I32, (n, n), 0)
    c = lax.broadcasted_iota(I32, (n, n), 1)
    return (r >= c).astype(BF16)


def _sb_prompt_kernel(q_ref, k_ref, v_ref, o_ref):
    qb = pl.program_id(2)
    q = q_ref[...]
    head0 = lax.broadcasted_iota(I32, (1, PAIR), 1) < HEAD_DIM
    r2 = lax.broadcasted_iota(I32, (2 * KB, 2 * KB), 0)
    c2 = lax.broadcasted_iota(I32, (2 * KB, 2 * KB), 1)
    cum_u = jnp.logical_and(r2 >= c2, (r2 < KB) == (c2 < KB)).astype(BF16)
    r = lax.broadcasted_iota(I32, (QB, 2 * KB), 0)
    c = lax.broadcasted_iota(I32, (QB, 2 * KB), 1)
    left = c < KB
    strictly_before = jnp.where(left, c, c - KB) < r

    def stacked(ref, kb):
        x = ref[pl.ds(pl.multiple_of(kb * KB, KB), KB), :]
        zero = jnp.zeros_like(x)
        return jnp.concatenate([jnp.where(head0, x, zero), jnp.where(head0, zero, x)], axis=0)

    def block(kb, c0, c1, acc, valid):
        z = _dot_nt(q, stacked(k_ref, kb))
        a, cs = _sb_weights(z, jnp.where(left, c0, c1), cum_u, valid)
        return (c0 + cs[:, 0:1], c1 + cs[:, KB:KB + 1],
                acc + _dot(a.astype(BF16), stacked(v_ref, kb)))

    def top(c0, c1):
        return jnp.max(jnp.maximum(c0, c1))

    zero_col = jnp.zeros((QB, 1), F32)
    c0, c1, acc = block(qb, zero_col, zero_col, jnp.zeros((QB, PAIR), F32), strictly_before)

    def cond(st):
        return jnp.logical_and(st[0] >= 0, st[1] > -SB_SKIP)

    def body(st):
        kb, _, c0, c1, acc = st
        c0, c1, acc = block(kb, c0, c1, acc, None)
        return kb - 1, top(c0, c1), c0, c1, acc

    st = lax.while_loop(cond, body, (qb - 1, top(c0, c1), c0, c1, acc))
    o_ref[...] = st[4]


def _sb_prompt(q1b, k1b, v1b, n_seq, seq_len):
    t_all, dh = q1b.shape
    n_pairs = dh // PAIR
    n_qb = seq_len // QB
    return pl.pallas_call(
        _sb_prompt_kernel,
        grid=(n_seq, n_pairs, n_qb),
        in_specs=[pl.BlockSpec((QB, PAIR), lambda n, j, i: (n * n_qb + i, j)),
                  pl.BlockSpec((seq_len, PAIR), lambda n, j, i: (n, j)),
                  pl.BlockSpec((seq_len, PAIR), lambda n, j, i: (n, j))],
        out_specs=pl.BlockSpec((QB, PAIR), lambda n, j, i: (n * n_qb + i, j)),
        out_shape=jax.ShapeDtypeStruct((t_all, dh), F32),
        compiler_params=_cparams(("parallel", "parallel", "arbitrary")),
        name="sb_prompt",
    )(q1b, k1b, v1b)


def _top_blocks(gate, n_past, blk_iota, axis):
    g = jnp.where(blk_iota < n_past, gate, NEG)
    big = jnp.int32(2 ** 30)
    sel = jnp.zeros(gate.shape, F32)
    for r in range(MOBA_TOPK):
        m = jnp.max(g, axis=axis, keepdims=True)
        idx = jnp.min(jnp.where(g == m, blk_iota, big), axis=axis, keepdims=True)
        pick = jnp.logical_and(blk_iota == idx, r < n_past)
        sel = jnp.where(pick, 1.0, sel)
        g = jnp.where(pick, 2.0 * NEG, g)
    return sel


def _softmax_step_t(s_t, v_t, m, l, acc_t):
    m_new = jnp.maximum(m, jnp.max(s_t, axis=0, keepdims=True))
    alpha = jnp.exp(m - m_new)
    p = jnp.exp(s_t - m_new)
    l = alpha * l + jnp.sum(p, axis=0, keepdims=True)
    acc_t = alpha * acc_t + _dot(v_t, p.astype(BF16))
    return m_new, l, acc_t


def _moba_prompt_kernel(slopes_ref, q_ref, k_ref, vt_ref, km_ref, o_ref, sel_ref, cb_ref):
    j = pl.program_id(1)
    qb = pl.program_id(2)
    q = q_ref[...]
    nq = q.shape[0]
    lane = lax.broadcasted_iota(I32, (1, PAIR), 1)
    zero = jnp.zeros_like(q)
    qq = jnp.concatenate([jnp.where(lane < HEAD_DIM, q, zero), jnp.where(lane < HEAD_DIM, zero, q)],
                         axis=0)
    own = (qb * nq) // MOBA_BLOCK
    q_off = qb * nq - own * MOBA_BLOCK
    nb = km_ref.shape[0]
    span = MOBA_GROUP * MOBA_BLOCK
    blk_iota = lax.broadcasted_iota(I32, (nb, 2 * nq), 0)
    km_hi, km_lo = _split_bf16(km_ref[...])
    sel_ref[...] = _top_blocks(_dot_nt(km_hi, qq) + _dot_nt(km_lo, qq), own, blk_iota, 0)
    key = lax.broadcasted_iota(I32, (span, nq), 0).astype(F32)
    slopes = [slopes_ref[2 * j + h] for h in range(2)]
    for h in range(2):
        cb_ref[h] = slopes[h] * key

    def head_cols(x, h):
        return x[:, h * nq:(h + 1) * nq]

    def body(g, st):
        start = pl.multiple_of(g * span, span)
        qk = _dot_nt(k_ref[pl.ds(start, span), :], qq)
        base = (g * span - qb * nq).astype(F32)
        ps, out = [], []
        for h in range(2):
            m, l = st[3 * h], st[3 * h + 1]
            tiles = []
            for b in range(MOBA_GROUP):
                rows = slice(b * MOBA_BLOCK, (b + 1) * MOBA_BLOCK)
                picked = head_cols(sel_ref[pl.ds(g * MOBA_GROUP + b, 1), :], h)
                row = jnp.where(picked > 0.0, slopes[h] * base, NEG)
                tiles.append(head_cols(qk[rows], h) + cb_ref[h, rows, :] + row)
            s_t = jnp.concatenate(tiles, axis=0)
            m_new = jnp.maximum(m, jnp.max(s_t, axis=0, keepdims=True))
            alpha = jnp.exp(m - m_new)
            p = jnp.exp(s_t - m_new)
            ps.append(p.astype(BF16))
            out.append((m_new, alpha * l + jnp.sum(p, axis=0, keepdims=True), alpha))
        pv = _dot(vt_ref[:, pl.ds(start, span)], jnp.concatenate(ps, axis=1))
        new = []
        for h in range(2):
            m_new, l_new, alpha = out[h]
            d_rows = slice(h * HEAD_DIM, (h + 1) * HEAD_DIM)
            new.extend((m_new, l_new, alpha * st[3 * h + 2] + head_cols(pv[d_rows], h)))
        return tuple(new)

    init = (jnp.full((1, nq), NEG, F32), jnp.zeros((1, nq), F32), jnp.zeros((HEAD_DIM, nq), F32)) * 2
    st = lax.fori_loop(0, (own + MOBA_GROUP - 1) // MOBA_GROUP, body, init)

    start = pl.multiple_of(own * MOBA_BLOCK, MOBA_BLOCK)
    k = k_ref[pl.ds(start, MOBA_BLOCK), :]
    vt = vt_ref[:, pl.ds(start, MOBA_BLOCK)]
    qk = _dot_nt(k, qq)
    blk = slice(0, MOBA_BLOCK)
    causal = (lax.broadcasted_iota(I32, (MOBA_BLOCK, nq), 0)
              <= lax.broadcasted_iota(I32, (MOBA_BLOCK, nq), 1) + q_off)
    res = []
    for h in range(2):
        m, l, acc = st[3 * h:3 * h + 3]
        s_t = head_cols(qk, h) + cb_ref[h, blk, :] - slopes[h] * q_off.astype(F32)
        s_t = jnp.where(causal, s_t, NEG)
        m, l, acc = _softmax_step_t(s_t, vt[h * HEAD_DIM:(h + 1) * HEAD_DIM], m, l, acc)
        res.append(acc / l)
    o_ref[...] = jnp.concatenate(res, axis=0).T


def _moba_prompt(slopes, q2b, k2b, v2t, km, n_seq, seq_len):
    t_all, dh = q2b.shape
    n_pairs = dh // PAIR
    nq = MOBA_QB
    n_qb = seq_len // nq
    nb = km.shape[0] // n_seq
    span = MOBA_GROUP * MOBA_BLOCK
    assert seq_len % span == 0 and nb % MOBA_GROUP == 0 and MOBA_BLOCK % nq == 0
    return pl.pallas_call(
        _moba_prompt_kernel,
        grid=(n_seq, n_pairs, n_qb),
        in_specs=[pl.BlockSpec(memory_space=pltpu.SMEM),
                  pl.BlockSpec((nq, PAIR), lambda n, j, i: (n * n_qb + i, j)),
                  pl.BlockSpec((seq_len, PAIR), lambda n, j, i: (n, j)),
                  pl.BlockSpec((PAIR, seq_len), lambda n, j, i: (j, n)),
                  pl.BlockSpec((nb, PAIR), lambda n, j, i: (n, j))],
        out_specs=pl.BlockSpec((nq, PAIR), lambda n, j, i: (n * n_qb + i, j)),
        out_shape=jax.ShapeDtypeStruct((t_all, dh), F32),
        scratch_shapes=[pltpu.VMEM((nb, 2 * nq), F32), pltpu.VMEM((2, span, nq), F32)],
        compiler_params=_cparams(("parallel", "parallel", "arbitrary")),
        name="moba_prompt",
    )(slopes, q2b, k2b, v2t, km)


def _head_rows(q, n_heads):
    s, dh = q.shape
    rows = jnp.concatenate([q] * n_heads, axis=0)
    rr = lax.broadcasted_iota(I32, (n_heads * s, dh), 0) // s
    ll = lax.broadcasted_iota(I32, (n_heads * s, dh), 1) // HEAD_DIM
    keep = rr == ll
    return jnp.where(keep, rows, 0.0), keep


def _merge_head_rows(acc, keep, n_heads):
    s = acc.shape[0] // n_heads
    masked = jnp.where(keep, acc, 0.0)
    out = masked[0:s]
    for h in range(1, n_heads):
        out = out + masked[h * s:(h + 1) * s]
    return out


def _sb_sample_kernel(pt_ref, q_ref, kn_ref, vn_ref, kc_hbm, vc_hbm, o_ref, kbuf, vbuf, sem,
                      *, n_heads, n_pages):
    b = pl.program_id(0)
    s, dh = q_ref.shape

    def slot_of(p):
        return (n_pages - 1 - p) % 2

    def copies(p):
        pg = pt_ref[b, p]
        slot = slot_of(p)
        return (pltpu.make_async_copy(kc_hbm.at[pg], kbuf.at[slot], sem.at[0, slot]),
                pltpu.make_async_copy(vc_hbm.at[pg], vbuf.at[slot], sem.at[1, slot]))

    def start(p):
        for cp in copies(p):
            cp.start()

    def wait(p):
        for cp in copies(p):
            cp.wait()

    start(n_pages - 1)
    start(n_pages - 2)
    qrows, keep = _head_rows(q_ref[...], n_heads)
    qrows = qrows.astype(BF16)
    rows = n_heads * s
    cum_u = _cum_matrix(KB)
    r = lax.broadcasted_iota(I32, (rows, KB), 0) % s
    c = lax.broadcasted_iota(I32, (rows, KB), 1)
    carry, acc = _sb_block(qrows, kn_ref[...], vn_ref[...], jnp.zeros((rows, 1), F32),
                           jnp.zeros((rows, dh), F32), cum_u, c < r)

    def cond(st):
        return jnp.logical_and(st[0] >= 0, st[1] > -SB_SKIP)

    def body(st):
        p, _, carry, acc = st
        wait(p)
        slot = slot_of(p)
        a, cs = _sb_weights(_dot(qrows, kbuf[slot].astype(BF16)), carry, cum_u, None)
        carry = carry + cs[:, 0:1]
        acc = acc + _dot_nt(a.astype(BF16), vbuf[slot].astype(BF16))

        @pl.when(p >= 2)
        def _():
            start(p - 2)

        return p - 1, jnp.max(carry), carry, acc

    st = lax.while_loop(cond, body, (jnp.int32(n_pages - 1), jnp.max(carry), carry, acc))
    p_exit = st[0]

    @pl.when(p_exit >= 0)
    def _():
        wait(p_exit)

    @pl.when(p_exit >= 1)
    def _():
        wait(p_exit - 1)

    o_ref[...] = _merge_head_rows(st[3], keep, n_heads)


def _sb_sample(page_table, q1s, k_new, v_new, cache_kt, cache_vt):
    n_b, n_pages = page_table.shape
    dh = q1s.shape[1]
    s = q1s.shape[0] // n_b
    page = cache_kt.shape[2]
    assert page == KB and n_pages >= 2
    n_heads = dh // HEAD_DIM
    per_b = lambda b, pt: (b, 0)
    grid_spec = pltpu.PrefetchScalarGridSpec(
        num_scalar_prefetch=1,
        grid=(n_b,),
        in_specs=[pl.BlockSpec((s, dh), per_b), pl.BlockSpec((KB, dh), per_b),
                  pl.BlockSpec((KB, dh), per_b), pl.BlockSpec(memory_space=pl.ANY),
                  pl.BlockSpec(memory_space=pl.ANY)],
        out_specs=pl.BlockSpec((s, dh), per_b),
        scratch_shapes=[pltpu.VMEM((2, dh, page), F32), pltpu.VMEM((2, dh, page), F32),
                        pltpu.SemaphoreType.DMA((2, 2))],
    )
    return pl.pallas_call(
        functools.partial(_sb_sample_kernel, n_heads=n_heads, n_pages=n_pages),
        grid_spec=grid_spec,
        out_shape=jax.ShapeDtypeStruct(q1s.shape, F32),
        compiler_params=_cparams(("arbitrary",)),
        name="sb_sample",
    )(page_table, q1s, k_new, v_new, cache_kt, cache_vt)


def _blocks_per_step(n_blk):
    return max(g for g in (8, 4, 2, 1) if n_blk % g == 0)


def _moba_pick_kernel(pt_ref, q_ref, *refs, n_heads, n_pg):
    del pt_ref
    k_refs, (sel_ref, gate_ref, qt_ref) = refs[:n_pg], refs[n_pg:]
    step = pl.program_id(1)
    n_steps = pl.num_programs(1)
    per_step = n_pg // 2
    rows_p = gate_ref.shape[1]

    @pl.when(step == 0)
    def _():
        qrows, _ = _head_rows(q_ref[...], n_heads)
        pad = jnp.zeros((rows_p - qrows.shape[0], qrows.shape[1]), F32)
        qt_ref[...] = jnp.concatenate([qrows, pad], axis=0).T
        gate_ref[...] = jnp.zeros(gate_ref.shape, F32)

    qt = qt_ref[...]
    for bb in range(per_step):
        ksum = (jnp.sum(k_refs[2 * bb][0], axis=1, keepdims=True)
                + jnp.sum(k_refs[2 * bb + 1][0], axis=1, keepdims=True))
        gate_ref[pl.ds(step * per_step + bb, 1), :] = jnp.sum(
            qt * (ksum * (1.0 / MOBA_BLOCK)), axis=0, keepdims=True)

    @pl.when(step == n_steps - 1)
    def _():
        blk_iota = lax.broadcasted_iota(I32, gate_ref.shape, 0)
        sel_t = _top_blocks(gate_ref[...], n_steps * per_step, blk_iota, 0)
        sel_ref[0] = sel_t.T[0:sel_ref.shape[1]]


def _moba_pick(page_table, q2s, cache_kt, nbp):
    n_b, n_pages = page_table.shape
    dh = q2s.shape[1]
    s = q2s.shape[0] // n_b
    page = cache_kt.shape[2]
    n_heads = dh // HEAD_DIM
    rows = n_heads * s
    n_blk = n_pages // 2
    assert 2 * page == MOBA_BLOCK and nbp == PAIR and rows <= PAIR and n_blk <= nbp
    per_step = _blocks_per_step(n_blk)
    n_pg = 2 * per_step

    def pg(i):
        return lambda b, m, pt: (pt[b, n_pg * m + i], 0, 0)

    grid_spec = pltpu.PrefetchScalarGridSpec(
        num_scalar_prefetch=1,
        grid=(n_b, n_blk // per_step),
        in_specs=[pl.BlockSpec((s, dh), lambda b, m, pt: (b, 0))]
        + [pl.BlockSpec((1, dh, page), pg(i)) for i in range(n_pg)],
        out_specs=pl.BlockSpec((1, rows, nbp), lambda b, m, pt: (b, 0, 0)),
        scratch_shapes=[pltpu.VMEM((nbp, PAIR), F32), pltpu.VMEM((dh, PAIR), F32)],
    )
    return pl.pallas_call(
        functools.partial(_moba_pick_kernel, n_heads=n_heads, n_pg=n_pg),
        grid_spec=grid_spec,
        out_shape=jax.ShapeDtypeStruct((n_b, rows, nbp), F32),
        compiler_params=_cparams(("parallel", "arbitrary")),
        name="moba_pick",
    )(page_table, q2s, *([cache_kt] * n_pg))


def _moba_sample_kernel(pt_ref, q_ref, slope_ref, sel_ref, kn_ref, vn_ref, *refs,
                        n_heads, past_len, n_pg):
    del pt_ref
    k_refs, v_refs = refs[:n_pg], refs[n_pg:2 * n_pg]
    o_ref, m_ref, l_ref, acc_ref = refs[2 * n_pg:]
    step = pl.program_id(1)
    n_steps = pl.num_programs(1) - 1
    per_step = n_pg // 2
    s = q_ref.shape[0]
    rows = n_heads * s
    qrows, keep = _head_rows(q_ref[...], n_heads)
    qrows = qrows.astype(BF16)
    nbp = sel_ref.shape[2]
    page = k_refs[0].shape[2]
    span = n_pg * page
    blk_iota = lax.broadcasted_iota(I32, (rows, nbp), 1)
    slope = slope_ref[:, 0:1]

    @pl.when(step == 0)
    def _():
        m_ref[...] = jnp.full(m_ref.shape, NEG, F32)
        l_ref[...] = jnp.zeros(l_ref.shape, F32)
        acc_ref[...] = jnp.zeros(acc_ref.shape, F32)

    def update(qk, ok, rel, pv):
        sc = jnp.where(ok, qk - slope * rel, NEG)
        m = m_ref[:, 0:1]
        m_new = jnp.maximum(m, jnp.max(sc, axis=-1, keepdims=True))
        alpha = jnp.exp(m - m_new)
        p = jnp.where(ok, jnp.exp(sc - m_new), 0.0)
        l = alpha * l_ref[:, 0:1] + jnp.sum(p, axis=-1, keepdims=True)
        acc = alpha * acc_ref[...] + pv(p.astype(BF16))
        m_ref[...] = jnp.broadcast_to(m_new, m_ref.shape)
        l_ref[...] = jnp.broadcast_to(l, l_ref.shape)
        acc_ref[...] = acc
        return l, acc

    def rel_pos(width):
        r = lax.broadcasted_iota(I32, (rows, width), 0) % s
        c = lax.broadcasted_iota(I32, (rows, width), 1)
        return r - c

    @pl.when(step < n_steps)
    def _():
        qk = jnp.concatenate([_dot(qrows, kr[0].astype(BF16)) for kr in k_refs], axis=1)
        sel = sel_ref[0]
        oks = []
        for bb in range(per_step):
            picked = jnp.max(jnp.where(blk_iota == step * per_step + bb, sel, 0.0),
                             axis=-1, keepdims=True)
            oks.append(jnp.broadcast_to(picked, (rows, MOBA_BLOCK)) > 0.0)
        dist0 = (past_len - step * span).astype(F32)

        def pv(p):
            out = _dot_nt(p[:, 0:page], v_refs[0][0].astype(BF16))
            for i in range(1, n_pg):
                out = out + _dot_nt(p[:, i * page:(i + 1) * page], v_refs[i][0].astype(BF16))
            return out

        update(qk, jnp.concatenate(oks, axis=1), rel_pos(span).astype(F32) + dist0, pv)

    @pl.when(step == n_steps)
    def _():
        rel = rel_pos(MOBA_BLOCK)
        l, acc = update(_dot_nt(qrows, kn_ref[...]), rel >= 0, rel.astype(F32),
                        lambda p: _dot(p, vn_ref[...]))
        o_ref[...] = _merge_head_rows(acc / l, keep, n_heads)


def _moba_sample(page_table, q2s, slope_rows, sel, k_new, v_new, cache_kt, cache_vt):
    n_b, n_pages = page_table.shape
    dh = q2s.shape[1]
    s = q2s.shape[0] // n_b
    page = cache_kt.shape[2]
    n_blk = n_pages // 2
    n_heads = dh // HEAD_DIM
    rows = n_heads * s
    nbp = sel.shape[2]
    per_step = _blocks_per_step(n_blk)
    n_pg = 2 * per_step
    n_steps = n_blk // per_step

    def pg(i):
        return lambda b, m, pt: (pt[b, n_pg * jnp.minimum(m, n_steps - 1) + i], 0, 0)

    per_b = lambda b, m, pt: (b, 0)
    page_specs = [pl.BlockSpec((1, dh, page), pg(i)) for i in range(n_pg)]
    grid_spec = pltpu.PrefetchScalarGridSpec(
        num_scalar_prefetch=1,
        grid=(n_b, n_steps + 1),
        in_specs=[pl.BlockSpec((s, dh), per_b),
                  pl.BlockSpec((rows, PAIR), lambda b, m, pt: (0, 0)),
                  pl.BlockSpec((1, rows, nbp), lambda b, m, pt: (b, 0, 0)),
                  pl.BlockSpec((MOBA_BLOCK, dh), per_b),
                  pl.BlockSpec((MOBA_BLOCK, dh), per_b)] + page_specs + page_specs,
        out_specs=pl.BlockSpec((s, dh), per_b),
        scratch_shapes=[pltpu.VMEM((rows, PAIR), F32), pltpu.VMEM((rows, PAIR), F32),
                        pltpu.VMEM((rows, dh), F32)],
    )
    return pl.pallas_call(
        functools.partial(_moba_sample_kernel, n_heads=n_heads, past_len=n_pages * page, n_pg=n_pg),
        grid_spec=grid_spec,
        out_shape=jax.ShapeDtypeStruct(q2s.shape, F32),
        compiler_params=_cparams(("parallel", "arbitrary")),
        name="moba_sample",
    )(page_table, q2s, slope_rows, sel, k_new, v_new, *([cache_kt] * n_pg), *([cache_vt] * n_pg))


def _mix_route_kernel(cnt_in_ref, o1_ref, o2_ref, x_ref, g1_ref, g2_ref, wo_ref, gf_ref,
                      wr_ref, br_ref, h_ref, xn_ref, slab_ref, cnt_ref, run_ref):
    i = pl.program_id(0)

    @pl.when(i == 0)
    def _():
        run_ref[...] = cnt_in_ref[...]

    def rms(t, g):
        return t * lax.rsqrt(jnp.mean(t * t, axis=-1, keepdims=True) + RMS_EPS) * g

    n1 = rms(o1_ref[...], g1_ref[...]).astype(BF16)
    n2 = rms(o2_ref[...], g2_ref[...]).astype(BF16)
    dh = n1.shape[1]
    h = x_ref[...] + _dot(n1, wo_ref[0:dh, :]) + _dot(n2, wo_ref[dh:2 * dh, :])
    h_ref[...] = h
    xn = rms(h, gf_ref[...])
    xn_ref[...] = xn

    x_hi, x_lo = _split_bf16(xn)
    w_hi, w_lo = _split_bf16(wr_ref[...])
    logits = _dot(x_hi, w_hi) + _dot(x_hi, w_lo) + _dot(x_lo, w_hi) + br_ref[...]
    tl, nl = logits.shape
    lane = lax.broadcasted_iota(I32, (tl, nl), 1)
    big = jnp.int32(2 ** 30)

    def first_max(vals):
        m = jnp.max(vals, axis=-1, keepdims=True)
        idx = jnp.min(jnp.where(vals == m, lane, big), axis=-1, keepdims=True)
        return m, idx

    grp_logit = jnp.where(lane < N_GROUPS, logits, NEG)
    gm, grp = first_max(grp_logit)
    denom = jnp.sum(jnp.where(lane < N_GROUPS, jnp.exp(grp_logit - gm), 0.0), axis=-1, keepdims=True)
    p_top = 1.0 / denom
    e_lo = ROUTE_LANE0 + grp * EXPERTS_PER_GROUP
    in_grp = jnp.logical_and(lane >= e_lo, lane < e_lo + EXPERTS_PER_GROUP)
    e_logit = jnp.where(in_grp, logits, NEG)
    m1, i1 = first_max(e_logit)
    m2, i2 = first_max(jnp.where(lane == i1, 2.0 * NEG, e_logit))
    e2 = jnp.exp(m2 - m1)
    gate1 = p_top / (1.0 + e2)
    gate2 = p_top * e2 / (1.0 + e2)

    hot1 = lane == i1
    hot2 = lane == i2
    both = jnp.logical_or(hot1, hot2)
    rr = lax.broadcasted_iota(I32, (tl, tl), 0)
    cc = lax.broadcasted_iota(I32, (tl, tl), 1)
    earlier = (cc < rr).astype(BF16)
    before = _dot(earlier, jnp.where(both, 1.0, 0.0).astype(BF16)) + run_ref[...]
    rank1 = jnp.sum(jnp.where(hot1, before, 0.0), axis=-1, keepdims=True)
    rank2 = jnp.sum(jnp.where(hot2, before, 0.0), axis=-1, keepdims=True)
    run_ref[...] = run_ref[...] + jnp.sum(jnp.where(both, 1.0, 0.0), axis=0, keepdims=True)
    cnt_ref[...] = run_ref[...]

    eid1 = (i1 - ROUTE_LANE0).astype(F32)
    eid2 = (i2 - ROUTE_LANE0).astype(F32)
    cols = (eid1, eid2, rank1, rank2, gate1, gate2)
    slab = jnp.zeros((tl, nl), F32)
    for n, col in enumerate(cols):
        slab = jnp.where(lane == n, col, slab)
    slab_ref[...] = slab


def _mix_route(cnt_in, o1, o2, x, g1, g2, w_out_b, g_ffn, w_route, b_route):
    t, d = x.shape
    dh = o1.shape[1]
    tt = _tok_tile(t)
    tile = lambda i: (i, 0)
    const = lambda i: (0, 0)
    return pl.pallas_call(
        _mix_route_kernel,
        grid=(t // tt,),
        in_specs=[pl.BlockSpec((1, PAIR), const),
                  pl.BlockSpec((tt, dh), tile),
                  pl.BlockSpec((tt, dh), tile),
                  pl.BlockSpec((tt, d), tile),
                  pl.BlockSpec((1, dh), const), pl.BlockSpec((1, dh), const),
                  pl.BlockSpec((d, d), const), pl.BlockSpec((1, d), const),
                  pl.BlockSpec((d, PAIR), const), pl.BlockSpec((1, PAIR), const)],
        out_specs=[pl.BlockSpec((tt, d), tile), pl.BlockSpec((tt, d), tile),
                   pl.BlockSpec((tt, PAIR), tile), pl.BlockSpec((1, PAIR), const)],
        out_shape=[jax.ShapeDtypeStruct((t, d), F32), jax.ShapeDtypeStruct((t, d), F32),
                   jax.ShapeDtypeStruct((t, PAIR), F32), jax.ShapeDtypeStruct((1, PAIR), F32)],
        scratch_shapes=[pltpu.VMEM((1, PAIR), F32)],
        compiler_params=_cparams(("arbitrary",)),
        name="mix_route",
    )(cnt_in, o1, o2, x, g1[None], g2[None], w_out_b, g_ffn[None], w_route, b_route)


def _row_copy(src, src_row, dst, dst_row, sem):
    return pltpu.make_async_copy(src.at[pl.ds(src_row, 1)], dst.at[pl.ds(dst_row, 1)], sem)


def _load_route(route_hbm, route_smem, sem):
    cp = pltpu.make_async_copy(route_hbm.at[pl.program_id(0)], route_smem, sem)
    cp.start()
    cp.wait()


def _dispatch_kernel(start_ref, route_hbm, xn_ref, xs_in, xs_out, route_smem, rsem, sem):
    del xs_in
    _load_route(route_hbm, route_smem, rsem)
    tl = xn_ref.shape[0]

    def slot(i, k):
        return start_ref[route_smem[4 * i + k]] + route_smem[4 * i + 2 + k]

    def issue(i, carry):
        _row_copy(xn_ref, i, xs_out, slot(i, 0), sem).start(priority=0)
        _row_copy(xn_ref, i, xs_out, slot(i, 1), sem).start(priority=1)
        return carry

    def drain(i, carry):
        _row_copy(xn_ref, 0, xs_out, 0, sem).wait()
        _row_copy(xn_ref, 0, xs_out, 0, sem).wait()
        return carry

    lax.fori_loop(0, tl, issue, 0, unroll=8)
    lax.fori_loop(0, tl, drain, 0, unroll=True)


def _dispatch(pad_start, route_i, xn, xs):
    t, d = xn.shape
    tt = _tok_tile(t)
    grid_spec = pltpu.PrefetchScalarGridSpec(
        num_scalar_prefetch=1,
        grid=(t // tt,),
        in_specs=[pl.BlockSpec(memory_space=pl.ANY),
                  pl.BlockSpec((tt, d), lambda i, st: (i, 0)),
                  pl.BlockSpec(memory_space=pl.ANY)],
        out_specs=pl.BlockSpec(memory_space=pl.ANY),
        scratch_shapes=[pltpu.SMEM((4 * tt,), I32), pltpu.SemaphoreType.DMA,
                        pltpu.SemaphoreType.DMA],
    )
    return pl.pallas_call(
        _dispatch_kernel,
        grid_spec=grid_spec,
        out_shape=jax.ShapeDtypeStruct(xs.shape, xs.dtype),
        input_output_aliases={3: 0},
        compiler_params=_cparams(("arbitrary",)),
        name="dispatch",
    )(pad_start, route_i, xn, xs)


def _expert_kernel(be_ref, x_ref, wg_ref, wu_ref, wd_ref, y_ref):
    del be_ref
    x = x_ref[...].astype(BF16)
    g = _dot(x, wg_ref[0])
    u = _dot(x, wu_ref[0])
    hid = g * (1.0 / (1.0 + jnp.exp(-g))) * u
    y_ref[...] = _dot(hid.astype(BF16), wd_ref[0])


def _experts(block_expert, xs, wg_b, wu_b, wd_b):
    n_slots, d = xs.shape
    de = wg_b.shape[2]
    n_blocks = n_slots // SLOT_BLOCK
    grid_spec = pltpu.PrefetchScalarGridSpec(
        num_scalar_prefetch=1,
        grid=(n_blocks,),
        in_specs=[pl.BlockSpec((SLOT_BLOCK, d), lambda b, be: (b, 0)),
                  pl.BlockSpec((1, d, de), lambda b, be: (be[b], 0, 0)),
                  pl.BlockSpec((1, d, de), lambda b, be: (be[b], 0, 0)),
                  pl.BlockSpec((1, de, d), lambda b, be: (be[b], 0, 0))],
        out_specs=pl.BlockSpec((SLOT_BLOCK, d), lambda b, be: (b, 0)),
    )
    return pl.pallas_call(
        _expert_kernel,
        grid_spec=grid_spec,
        out_shape=jax.ShapeDtypeStruct((n_slots, d), F32),
        compiler_params=_cparams(("arbitrary",)),
        name="experts",
    )(block_expert, xs, wg_b, wu_b, wd_b)


def _combine_kernel(start_ref, route_hbm, h_ref, slab_ref, ys_hbm, out_ref,
                    route_smem, y_buf, rsem, sem):
    _load_route(route_hbm, route_smem, rsem)
    tl = h_ref.shape[0]

    def slot(i, k):
        return start_ref[route_smem[4 * i + k]] + route_smem[4 * i + 2 + k]

    def issue(i, carry):
        _row_copy(ys_hbm, slot(i, 0), y_buf.at[0], i, sem).start(priority=0)
        _row_copy(ys_hbm, slot(i, 1), y_buf.at[1], i, sem).start(priority=1)
        return carry

    def drain(i, carry):
        _row_copy(ys_hbm, 0, y_buf.at[0], 0, sem).wait()
        _row_copy(ys_hbm, 0, y_buf.at[1], 0, sem).wait()
        return carry

    lax.fori_loop(0, tl, issue, 0, unroll=8)
    lax.fori_loop(0, tl, drain, 0, unroll=True)
    slab = slab_ref[...]
    out_ref[...] = h_ref[...] + slab[:, 4:5] * y_buf[0] + slab[:, 5:6] * y_buf[1]


def _combine(pad_start, route_i, h, slab, ys):
    t, d = h.shape
    tt = _tok_tile(t)
    grid_spec = pltpu.PrefetchScalarGridSpec(
        num_scalar_prefetch=1,
        grid=(t // tt,),
        in_specs=[pl.BlockSpec(memory_space=pl.ANY),
                  pl.BlockSpec((tt, d), lambda i, st: (i, 0)),
                  pl.BlockSpec((tt, PAIR), lambda i, st: (i, 0)),
                  pl.BlockSpec(memory_space=pl.ANY)],
        out_specs=pl.BlockSpec((tt, d), lambda i, st: (i, 0)),
        scratch_shapes=[pltpu.SMEM((4 * tt,), I32), pltpu.VMEM((2, tt, d), F32),
                        pltpu.SemaphoreType.DMA, pltpu.SemaphoreType.DMA],
    )
    return pl.pallas_call(
        _combine_kernel,
        grid_spec=grid_spec,
        out_shape=jax.ShapeDtypeStruct((t, d), F32),
        compiler_params=_cparams(("arbitrary",)),
        name="combine",
    )(pad_start, route_i, h, slab, ys)


def _route_ints(slab):
    t = slab.shape[0]
    tt = _tok_tile(t)
    return slab[:, 0:4].astype(I32).reshape(t // tt, 4 * tt)


def _pad_rows(x, n_b, rows):
    s = x.shape[0] // n_b
    x = x.reshape(n_b, s, x.shape[1])
    return jnp.pad(x, ((0, 0), (0, rows - s), (0, 0))).reshape(n_b * rows, -1)


def _layer(xp, xs, cache_k_sb, cache_v_sb, cache_k_mb, cache_v_mb, page_table,
           norm_attn, w_in, q_norm_mb, k_norm_mb, out_norm_sb, out_norm_mb, w_out,
           norm_ffn, w_rg, b_rg, w_re, b_re, w_gate, w_up, w_down):
    n_p, s_p, d = xp.shape
    n_s, s_s, _ = xs.shape
    t_p, t_s = n_p * s_p, n_s * s_s
    t_all = t_p + t_s
    dh = w_in.shape[1] // 6
    n_heads = dh // HEAD_DIM
    assert s_p % MOBA_BLOCK == 0 and dh % PAIR == 0
    xp2, xs2 = xp.reshape(t_p, d), xs.reshape(t_s, d)
    w_in_b = w_in.astype(BF16)
    slopes = 2.0 ** (-8.0 * jnp.arange(1, n_heads + 1, dtype=F32) / n_heads)

    pp = _project(xp2, n_p, norm_attn, w_in_b, q_norm_mb, k_norm_mb)
    ps = _project(xs2, 1, norm_attn, w_in_b, q_norm_mb, k_norm_mb)
    (k1p, v1p, k2p, v2p, _, _, q1pb, k1pb, v1pb, q2pb, k2pb, _, v2ptb, kmp) = pp
    (k1s, v1s, k2s, v2s, q1s, q2s, _, k1sb, v1sb, _, k2sb, v2sb, _, _) = ps

    nb = s_p // MOBA_BLOCK
    nb8 = -(-nb // 8) * 8
    kmp = jnp.pad(kmp.reshape(n_p, nb, dh), ((0, 0), (0, nb8 - nb), (0, 0))).reshape(n_p * nb8, dh)
    o_sb_p = _sb_prompt(q1pb, k1pb, v1pb, n_p, s_p)
    o_mb_p = _moba_prompt(slopes, q2pb, k2pb, v2ptb, kmp, n_p, s_p)

    pool, page = cache_k_sb.shape[0], cache_k_sb.shape[1]
    as_t = lambda c: c.transpose(0, 2, 3, 1).reshape(pool, dh, page)
    o_sb_s = _sb_sample(page_table, q1s, _pad_rows(k1sb, n_s, KB), _pad_rows(v1sb, n_s, KB),
                        as_t(cache_k_sb), as_t(cache_v_sb))
    c_k_mb, c_v_mb = as_t(cache_k_mb), as_t(cache_v_mb)
    n_blk_s = page_table.shape[1] * page // MOBA_BLOCK
    nbp_s = -(-n_blk_s // PAIR) * PAIR
    sel_s = _moba_pick(page_table, q2s, c_k_mb, nbp_s)
    slope_rows = jnp.broadcast_to(jnp.repeat(slopes, s_s)[:, None], (n_heads * s_s, PAIR))
    o_mb_s = _moba_sample(page_table, q2s, slope_rows, sel_s, _pad_rows(k2sb, n_s, MOBA_BLOCK),
                          _pad_rows(v2sb, n_s, MOBA_BLOCK), c_k_mb, c_v_mb)

    w_out_b = w_out.astype(BF16)
    w_route = jnp.pad(jnp.concatenate([w_rg, w_re], axis=1), ((0, 0), (0, PAIR - N_GROUPS - N_EXPERTS)))
    b_route = jnp.pad(jnp.concatenate([b_rg, b_re]), (0, PAIR - N_GROUPS - N_EXPERTS))[None]
    cnt0 = jnp.zeros((1, PAIR), F32)
    hp, xnp_, slab_p, cnt1 = _mix_route(cnt0, o_sb_p, o_mb_p, xp2, out_norm_sb, out_norm_mb,
                                        w_out_b, norm_ffn, w_route, b_route)
    hs, xns, slab_s, cnt2 = _mix_route(cnt1, o_sb_s, o_mb_s, xs2, out_norm_sb, out_norm_mb,
                                       w_out_b, norm_ffn, w_route, b_route)

    counts = cnt2[0, ROUTE_LANE0:ROUTE_LANE0 + N_EXPERTS].astype(I32)
    padded = (counts + SLOT_BLOCK - 1) // SLOT_BLOCK * SLOT_BLOCK
    pad_end = jnp.cumsum(padded)
    pad_start = (pad_end - padded).astype(I32)
    n_blocks = -(-2 * t_all // SLOT_BLOCK) + N_EXPERTS
    block_row0 = jnp.arange(n_blocks, dtype=I32) * SLOT_BLOCK
    block_expert = jnp.minimum(
        jnp.sum((pad_end[None, :] <= block_row0[:, None]).astype(I32), axis=1), N_EXPERTS - 1)

    route_p, route_s = _route_ints(slab_p), _route_ints(slab_s)
    x_sorted = jnp.zeros((n_blocks * SLOT_BLOCK, d), F32)
    x_sorted = _dispatch(pad_start, route_p, xnp_, x_sorted)
    x_sorted = _dispatch(pad_start, route_s, xns, x_sorted)
    y_sorted = _experts(block_expert, x_sorted, w_gate.astype(BF16), w_up.astype(BF16),
                        w_down.astype(BF16))
    yp = _combine(pad_start, route_p, hp, slab_p, y_sorted)
    ys = _combine(pad_start, route_s, hs, slab_s, y_sorted)

    h4 = lambda a, n, s: a.reshape(n, n_heads, HEAD_DIM, s).transpose(0, 3, 1, 2)
    new_kv = (h4(k1p, n_p, s_p), h4(v1p, n_p, s_p), h4(k2p, n_p, s_p), h4(v2p, n_p, s_p),
              h4(k1s, 1, t_s).reshape(n_s, s_s, n_heads, HEAD_DIM),
              h4(v1s, 1, t_s).reshape(n_s, s_s, n_heads, HEAD_DIM),
              h4(k2s, 1, t_s).reshape(n_s, s_s, n_heads, HEAD_DIM),
              h4(v2s, 1, t_s).reshape(n_s, s_s, n_heads, HEAD_DIM))
    return yp.reshape(n_p, s_p, d), ys.reshape(n_s, s_s, d), new_kv


def kernel(x_prompt, x_sample, cache_k_sb, cache_v_sb, cache_k_mb, cache_v_mb, page_table, norm_attn, w_in, q_norm_mb, k_norm_mb, out_norm_sb, out_norm_mb, w_out, norm_ffn, w_router_group, b_router_group, w_router_expert, b_router_expert, w_gate, w_up, w_down):
    depth = w_in.shape[0]
    xp, xs = x_prompt, x_sample
    kv = []
    for l in range(depth):
        xp, xs, new_kv = _layer(
            xp, xs, cache_k_sb[l], cache_v_sb[l], cache_k_mb[l], cache_v_mb[l], page_table,
            norm_attn[l], w_in[l], q_norm_mb[l], k_norm_mb[l], out_norm_sb[l], out_norm_mb[l],
            w_out[l], norm_ffn[l], w_router_group[l], b_router_group[l], w_router_expert[l],
            b_router_expert[l], w_gate[l], w_up[l], w_down[l])
        kv.append(new_kv)
    stacked = tuple(jnp.stack([layer_kv[i] for layer_kv in kv]) for i in range(8))
    return (xp, xs) + stacked
```

```python
import functools
import math

import jax
import jax.numpy as jnp
from jax import lax
from jax.experimental import pallas as pl
from jax.experimental.pallas import tpu as pltpu

F32 = jnp.float32
BF16 = jnp.bfloat16
I32 = jnp.int32

HEAD_DIM = 64
PAIR = 128
MOBA_BLOCK = 256
MOBA_TOPK = 3
MOBA_GROUP = 4
MOBA_QB = 256
N_GROUPS = 4
EXPERTS_PER_GROUP = 8
N_EXPERTS = N_GROUPS * EXPERTS_PER_GROUP
RMS_EPS = 1e-6
SCALE = HEAD_DIM ** -0.5
SB_QB = 256
KB = 128
TOK_TILE = 512
SLOT_BLOCK = 256
ROUTE_LANE0 = N_GROUPS
SB_SKIP = 90.0
NEG = -1e30
VMEM_LIMIT = 56 * 1024 * 1024


def _cparams(sem):
    return pltpu.CompilerParams(dimension_semantics=sem, vmem_limit_bytes=VMEM_LIMIT)


def _tok_tile(t):
    return math.gcd(t, TOK_TILE)


def _split_bf16(x):
    hi = x.astype(BF16)
    lo = (x - hi.astype(F32)).astype(BF16)
    return hi, lo


def _dot(a, b):
    return jnp.dot(a, b, preferred_element_type=F32)


def _dot_nt(a, b):
    return lax.dot_general(a, b, (((1,), (1,)), ((), ())), preferred_element_type=F32)


def _proj_kernel(x_ref, g_ref, w_ref, qn_ref, kn_ref, seg_ref,
                 k1t_ref, v1t_ref, k2t_ref, v2t_ref, q1s_ref, q2s_ref,
                 q1b_ref, k1b_ref, v1b_ref, q2b_ref, k2b_ref, v2b_ref, v2tb_ref, km_ref):
    x = x_ref[...]
    tl = x.shape[0]
    ms = jnp.mean(x * x, axis=-1, keepdims=True)
    xn = x * lax.rsqrt(ms + RMS_EPS) * g_ref[...]
    proj = _dot(xn.astype(BF16), w_ref[...])
    dh = q1s_ref.shape[1]
    seg = seg_ref[...]

    def headnorm(t, g):
        hi, lo = _split_bf16(t * t)
        ssum = _dot(hi, seg) + _dot(lo, seg)
        return t * lax.rsqrt(ssum * (1.0 / HEAD_DIM) + RMS_EPS) * g

    q1 = proj[:, 0 * dh:1 * dh]
    k1 = proj[:, 1 * dh:2 * dh]
    v1 = proj[:, 2 * dh:3 * dh]
    q2 = headnorm(proj[:, 3 * dh:4 * dh], qn_ref[...])
    k2 = headnorm(proj[:, 4 * dh:5 * dh], kn_ref[...])
    v2 = proj[:, 5 * dh:6 * dh]
    v2t = v2.T
    for ref, val in ((k1t_ref, k1.T), (v1t_ref, v1.T), (k2t_ref, k2.T), (v2t_ref, v2t)):
        ref[0] = val
    q1s = q1 * SCALE
    q2s = q2 * SCALE
    q1s_ref[...] = q1s
    q2s_ref[...] = q2s
    for bref, val in ((q1b_ref, q1s), (k1b_ref, k1), (v1b_ref, v1),
                      (q2b_ref, q2s), (k2b_ref, k2), (v2b_ref, v2)):
        bref[...] = val.astype(BF16)
    v2tb_ref[...] = v2t.astype(BF16)
    for c in range(tl // MOBA_BLOCK):
        blk = k2[c * MOBA_BLOCK:(c + 1) * MOBA_BLOCK]
        km_ref[c] = jnp.sum(blk, axis=0, keepdims=True) * (1.0 / MOBA_BLOCK)


def _project(x, n_seq, norm_g, w_in_b, q_norm, k_norm):
    t, d = x.shape
    s = t // n_seq
    dh = w_in_b.shape[1] // 6
    n_heads = dh // HEAD_DIM
    tt = _tok_tile(s)
    assert tt % MOBA_BLOCK == 0
    tps = s // tt
    r = jnp.arange(dh) // HEAD_DIM
    seg = (r[:, None] == r[None, :]).astype(BF16)
    qn = jnp.tile(q_norm, n_heads)[None]
    kn = jnp.tile(k_norm, n_heads)[None]
    tile = lambda i: (i, 0)
    const = lambda i: (0, 0)
    t_spec = pl.BlockSpec((1, dh, tt), lambda i: (i // tps, 0, i % tps))
    f_spec = pl.BlockSpec((tt, dh), tile)
    n_km = tt // MOBA_BLOCK
    return pl.pallas_call(
        _proj_kernel,
        grid=(t // tt,),
        in_specs=[pl.BlockSpec((tt, d), tile), pl.BlockSpec((1, d), const),
                  pl.BlockSpec(w_in_b.shape, const), pl.BlockSpec((1, dh), const),
                  pl.BlockSpec((1, dh), const), pl.BlockSpec((dh, dh), const)],
        out_specs=[t_spec] * 4 + [f_spec] * 8
        + [pl.BlockSpec((dh, tt), lambda i: (0, i)),
           pl.BlockSpec((n_km, 1, dh), lambda i: (i, 0, 0))],
        out_shape=[jax.ShapeDtypeStruct((n_seq, dh, s), F32)] * 4
        + [jax.ShapeDtypeStruct((t, dh), F32)] * 2
        + [jax.ShapeDtypeStruct((t, dh), BF16)] * 6
        + [jax.ShapeDtypeStruct((dh, t), BF16),
           jax.ShapeDtypeStruct((t // MOBA_BLOCK, 1, dh), F32)],
        compiler_params=_cparams(("parallel",)),
        name="in_proj",
    )(x, norm_g[None], w_in_b, qn, kn, seg)


def _sb_weights(z, carry, cum_u, valid):
    lk = -(jnp.maximum(z, 0.0) + jnp.log1p(jnp.exp(-jnp.abs(z))))
    if valid is not None:
        lk = jnp.where(valid, lk, 0.0)
    hi, lo = _split_bf16(lk)
    cs = _dot(hi, cum_u) + _dot(lo, cum_u)
    a = jnp.exp(z + cs + carry)
    if valid is not None:
        a = jnp.where(valid, a, 0.0)
    return a, cs


def _sb_block(qh, k, v, carry, acc, cum_u, valid):
    a, cs = _sb_weights(_dot_nt(qh, k), carry, cum_u, valid)
    return carry + cs[:, 0:1], acc + _dot(a.astype(BF16), v)


def _cum_matrix(n):
    r = lax.broadcasted_iota(I32, (n, n), 0)
    c = lax.broadcasted_iota(I32, (n, n), 1)
    return (r >= c).astype(BF16)


def _sb_prompt_kernel(q_ref, k_ref, v_ref, o_ref):
    qb = pl.program_id(2)
    q = q_ref[...]
    nq = q.shape[0]
    diag = nq // KB
    head0 = lax.broadcasted_iota(I32, (1, PAIR), 1) < HEAD_DIM
    r2 = lax.broadcasted_iota(I32, (2 * KB, 2 * KB), 0)
    c2 = lax.broadcasted_iota(I32, (2 * KB, 2 * KB), 1)
    cum_u = jnp.logical_and(r2 >= c2, (r2 < KB) == (c2 < KB)).astype(BF16)
    r = lax.broadcasted_iota(I32, (nq, 2 * KB), 0)
    c = lax.broadcasted_iota(I32, (nq, 2 * KB), 1)
    left = c < KB
    key = jnp.where(left, c, c - KB)

    def stacked(ref, kb):
        x = ref[pl.ds(pl.multiple_of(kb * KB, KB), KB), :]
        zero = jnp.zeros_like(x)
        return jnp.concatenate([jnp.where(head0, x, zero), jnp.where(head0, zero, x)], axis=0)

    def block(kb, c0, c1, acc, valid):
        z = _dot_nt(q, stacked(k_ref, kb))
        a, cs = _sb_weights(z, jnp.where(left, c0, c1), cum_u, valid)
        return (c0 + cs[:, 0:1], c1 + cs[:, KB:KB + 1],
                acc + _dot(a.astype(BF16), stacked(v_ref, kb)))

    def top(c0, c1):
        return jnp.max(jnp.maximum(c0, c1))

    c0 = c1 = jnp.zeros((nq, 1), F32)
    acc = jnp.zeros((nq, PAIR), F32)
    for d in reversed(range(diag)):
        c0, c1, acc = block(qb * diag + d, c0, c1, acc, key + d * KB < r)

    def cond(st):
        return jnp.logical_and(st[0] >= 0, st[1] > -SB_SKIP)

    def body(st):
        kb, _, c0, c1, acc = st
        c0, c1, acc = block(kb, c0, c1, acc, None)
        return kb - 1, top(c0, c1), c0, c1, acc

    st = lax.while_loop(cond, body, (qb * diag - 1, top(c0, c1), c0, c1, acc))
    o_ref[...] = st[4]


def _sb_prompt(q1b, k1b, v1b, n_seq, seq_len):
    t_all, dh = q1b.shape
    n_pairs = dh // PAIR
    assert SB_QB % KB == 0 and seq_len % SB_QB == 0
    n_qb = seq_len // SB_QB
    return pl.pallas_call(
        _sb_prompt_kernel,
        grid=(n_seq, n_pairs, n_qb),
        in_specs=[pl.BlockSpec((SB_QB, PAIR), lambda n, j, i: (n * n_qb + i, j)),
                  pl.BlockSpec((seq_len, PAIR), lambda n, j, i: (n, j)),
                  pl.BlockSpec((seq_len, PAIR), lambda n, j, i: (n, j))],
        out_specs=pl.BlockSpec((SB_QB, PAIR), lambda n, j, i: (n * n_qb + i, j)),
        out_shape=jax.ShapeDtypeStruct((t_all, dh), F32),
        compiler_params=_cparams(("parallel", "parallel", "arbitrary")),
        name="sb_prompt",
    )(q1b, k1b, v1b)


def _top_blocks(gate, n_past, blk_iota, axis):
    g = jnp.where(blk_iota < n_past, gate, NEG)
    big = jnp.int32(2 ** 30)
    sel = jnp.zeros(gate.shape, F32)
    for r in range(MOBA_TOPK):
        m = jnp.max(g, axis=axis, keepdims=True)
        idx = jnp.min(jnp.where(g == m, blk_iota, big), axis=axis, keepdims=True)
        pick = jnp.logical_and(blk_iota == idx, r < n_past)
        sel = jnp.where(pick, 1.0, sel)
        g = jnp.where(pick, 2.0 * NEG, g)
    return sel


def _softmax_step_t(s_t, v_t, m, l, acc_t):
    m_new = jnp.maximum(m, jnp.max(s_t, axis=0, keepdims=True))
    alpha = jnp.exp(m - m_new)
    p = jnp.exp(s_t - m_new)
    l = alpha * l + jnp.sum(p, axis=0, keepdims=True)
    acc_t = alpha * acc_t + _dot(v_t, p.astype(BF16))
    return m_new, l, acc_t


def _moba_prompt_kernel(slopes_ref, q_ref, k_ref, vt_ref, km_ref, o_ref, sel_ref, cb_ref):
    j = pl.program_id(1)
    qb = pl.program_id(2)
    q = q_ref[...]
    nq = q.shape[0]
    lane = lax.broadcasted_iota(I32, (1, PAIR), 1)
    zero = jnp.zeros_like(q)
    qq = jnp.concatenate([jnp.where(lane < HEAD_DIM, q, zero), jnp.where(lane < HEAD_DIM, zero, q)],
                         axis=0)
    own = (qb * nq) // MOBA_BLOCK
    q_off = qb * nq - own * MOBA_BLOCK
    nb = km_ref.shape[0]
    span = MOBA_GROUP * MOBA_BLOCK
    blk_iota = lax.broadcasted_iota(I32, (nb, 2 * nq), 0)
    km_hi, km_lo = _split_bf16(km_ref[...])
    sel_ref[...] = _top_blocks(_dot_nt(km_hi, qq) + _dot_nt(km_lo, qq), own, blk_iota, 0)
    slopes = [slopes_ref[2 * j + h] for h in range(2)]

    @pl.when(qb == 0)
    def _():
        key = lax.broadcasted_iota(I32, (span, nq), 0).astype(F32)
        for h in range(2):
            cb_ref[h] = slopes[h] * key

    def head_cols(x, h):
        return x[:, h * nq:(h + 1) * nq]

    def body(g, st):
        start = pl.multiple_of(g * span, span)
        qk = _dot_nt(k_ref[pl.ds(start, span), :], qq)
        base = (g * span - qb * nq).astype(F32)
        ps, out = [], []
        for h in range(2):
            m, l = st[3 * h], st[3 * h + 1]
            tiles = []
            for b in range(MOBA_GROUP):
                rows = slice(b * MOBA_BLOCK, (b + 1) * MOBA_BLOCK)
                picked = head_cols(sel_ref[pl.ds(g * MOBA_GROUP + b, 1), :], h)
                row = jnp.where(picked > 0.0, slopes[h] * base, NEG)
                tiles.append(head_cols(qk[rows], h) + cb_ref[h, rows, :] + row)
            s_t = jnp.concatenate(tiles, axis=0)
            m_new = jnp.maximum(m, jnp.max(s_t, axis=0, keepdims=True))
            alpha = jnp.exp(m - m_new)
            p = jnp.exp(s_t - m_new)
            ps.append(p.astype(BF16))
            out.append((m_new, alpha * l + jnp.sum(p, axis=0, keepdims=True), alpha))
        pv = _dot(vt_ref[:, pl.ds(start, span)], jnp.concatenate(ps, axis=1))
        new = []
        for h in range(2):
            m_new, l_new, alpha = out[h]
            d_rows = slice(h * HEAD_DIM, (h + 1) * HEAD_DIM)
            new.extend((m_new, l_new, alpha * st[3 * h + 2] + head_cols(pv[d_rows], h)))
        return tuple(new)

    init = (jnp.full((1, nq), NEG, F32), jnp.zeros((1, nq), F32), jnp.zeros((HEAD_DIM, nq), F32)) * 2
    st = lax.fori_loop(0, (own + MOBA_GROUP - 1) // MOBA_GROUP, body, init)

    start = pl.multiple_of(own * MOBA_BLOCK, MOBA_BLOCK)
    k = k_ref[pl.ds(start, MOBA_BLOCK), :]
    vt = vt_ref[:, pl.ds(start, MOBA_BLOCK)]
    qk = _dot_nt(k, qq)
    blk = slice(0, MOBA_BLOCK)
    causal = (lax.broadcasted_iota(I32, (MOBA_BLOCK, nq), 0)
              <= lax.broadcasted_iota(I32, (MOBA_BLOCK, nq), 1) + q_off)
    res = []
    for h in range(2):
        m, l, acc = st[3 * h:3 * h + 3]
        s_t = head_cols(qk, h) + cb_ref[h, blk, :] - slopes[h] * q_off.astype(F32)
        s_t = jnp.where(causal, s_t, NEG)
        m, l, acc = _softmax_step_t(s_t, vt[h * HEAD_DIM:(h + 1) * HEAD_DIM], m, l, acc)
        res.append(acc / l)
    o_ref[...] = jnp.concatenate(res, axis=0).T


def _moba_prompt(slopes, q2b, k2b, v2t, km, n_seq, seq_len):
    t_all, dh = q2b.shape
    n_pairs = dh // PAIR
    nq = MOBA_QB
    n_qb = seq_len // nq
    nb = km.shape[0] // n_seq
    span = MOBA_GROUP * MOBA_BLOCK
    assert seq_len % span == 0 and nb % MOBA_GROUP == 0 and MOBA_BLOCK % nq == 0
    return pl.pallas_call(
        _moba_prompt_kernel,
        grid=(n_seq, n_pairs, n_qb),
        in_specs=[pl.BlockSpec(memory_space=pltpu.SMEM),
                  pl.BlockSpec((nq, PAIR), lambda n, j, i: (n * n_qb + i, j)),
                  pl.BlockSpec((seq_len, PAIR), lambda n, j, i: (n, j)),
                  pl.BlockSpec((PAIR, seq_len), lambda n, j, i: (j, n)),
                  pl.BlockSpec((nb, PAIR), lambda n, j, i: (n, j))],
        out_specs=pl.BlockSpec((nq, PAIR), lambda n, j, i: (n * n_qb + i, j)),
        out_shape=jax.ShapeDtypeStruct((t_all, dh), F32),
        scratch_shapes=[pltpu.VMEM((nb, 2 * nq), F32), pltpu.VMEM((2, span, nq), F32)],
        compiler_params=_cparams(("parallel", "parallel", "arbitrary")),
        name="moba_prompt",
    )(slopes, q2b, k2b, v2t, km)


def _head_rows(q, n_heads):
    s, dh = q.shape
    rows = jnp.concatenate([q] * n_heads, axis=0)
    rr = lax.broadcasted_iota(I32, (n_heads * s, dh), 0) // s
    ll = lax.broadcasted_iota(I32, (n_heads * s, dh), 1) // HEAD_DIM
    keep = rr == ll
    return jnp.where(keep, rows, 0.0), keep


def _merge_head_rows(acc, keep, n_heads):
    s = acc.shape[0] // n_heads
    masked = jnp.where(keep, acc, 0.0)
    out = masked[0:s]
    for h in range(1, n_heads):
        out = out + masked[h * s:(h + 1) * s]
    return out


def _sb_sample_kernel(pt_ref, q_ref, kn_ref, vn_ref, kc_hbm, vc_hbm, o_ref, kbuf, vbuf, sem,
                      *, n_heads, n_pages):
    b = pl.program_id(0)
    s, dh = q_ref.shape

    def slot_of(p):
        return (n_pages - 1 - p) % 2

    def copies(p):
        pg = pt_ref[b, p]
        slot = slot_of(p)
        return (pltpu.make_async_copy(kc_hbm.at[pg], kbuf.at[slot], sem.at[0, slot]),
                pltpu.make_async_copy(vc_hbm.at[pg], vbuf.at[slot], sem.at[1, slot]))

    def start(p):
        for cp in copies(p):
            cp.start()

    def wait(p):
        for cp in copies(p):
            cp.wait()

    start(n_pages - 1)
    start(n_pages - 2)
    qrows, keep = _head_rows(q_ref[...], n_heads)
    qrows = qrows.astype(BF16)
    rows = n_heads * s
    cum_u = _cum_matrix(KB)
    r = lax.broadcasted_iota(I32, (rows, KB), 0) % s
    c = lax.broadcasted_iota(I32, (rows, KB), 1)
    carry, acc = _sb_block(qrows, kn_ref[...], vn_ref[...], jnp.zeros((rows, 1), F32),
                           jnp.zeros((rows, dh), F32), cum_u, c < r)

    def cond(st):
        return jnp.logical_and(st[0] >= 0, st[1] > -SB_SKIP)

    def body(st):
        p, _, carry, acc = st
        wait(p)
        slot = slot_of(p)
        a, cs = _sb_weights(_dot(qrows, kbuf[slot].astype(BF16)), carry, cum_u, None)
        carry = carry + cs[:, 0:1]
        acc = acc + _dot_nt(a.astype(BF16), vbuf[slot].astype(BF16))

        @pl.when(p >= 2)
        def _():
            start(p - 2)

        return p - 1, jnp.max(carry), carry, acc

    st = lax.while_loop(cond, body, (jnp.int32(n_pages - 1), jnp.max(carry), carry, acc))
    p_exit = st[0]

    @pl.when(p_exit >= 0)
    def _():
        wait(p_exit)

    @pl.when(p_exit >= 1)
    def _():
        wait(p_exit - 1)

    o_ref[...] = _merge_head_rows(st[3], keep, n_heads)


def _sb_sample(page_table, q1s, k_new, v_new, cache_kt, cache_vt):
    n_b, n_pages = page_table.shape
    dh = q1s.shape[1]
    s = q1s.shape[0] // n_b
    page = cache_kt.shape[2]
    assert page == KB and n_pages >= 2
    n_heads = dh // HEAD_DIM
    per_b = lambda b, pt: (b, 0)
    grid_spec = pltpu.PrefetchScalarGridSpec(
        num_scalar_prefetch=1,
        grid=(n_b,),
        in_specs=[pl.BlockSpec((s, dh), per_b), pl.BlockSpec((KB, dh), per_b),
                  pl.BlockSpec((KB, dh), per_b), pl.BlockSpec(memory_space=pl.ANY),
                  pl.BlockSpec(memory_space=pl.ANY)],
        out_specs=pl.BlockSpec((s, dh), per_b),
        scratch_shapes=[pltpu.VMEM((2, dh, page), F32), pltpu.VMEM((2, dh, page), F32),
                        pltpu.SemaphoreType.DMA((2, 2))],
    )
    return pl.pallas_call(
        functools.partial(_sb_sample_kernel, n_heads=n_heads, n_pages=n_pages),
        grid_spec=grid_spec,
        out_shape=jax.ShapeDtypeStruct(q1s.shape, F32),
        compiler_params=_cparams(("arbitrary",)),
        name="sb_sample",
    )(page_table, q1s, k_new, v_new, cache_kt, cache_vt)


def _blocks_per_step(n_blk):
    return max(g for g in (8, 4, 2, 1) if n_blk % g == 0)


def _moba_pick_kernel(pt_ref, q_ref, *refs, n_heads, n_pg):
    del pt_ref
    k_refs, (sel_ref, gate_ref, qt_ref) = refs[:n_pg], refs[n_pg:]
    step = pl.program_id(1)
    n_steps = pl.num_programs(1)
    per_step = n_pg // 2
    rows_p = gate_ref.shape[1]

    @pl.when(step == 0)
    def _():
        qrows, _ = _head_rows(q_ref[...], n_heads)
        pad = jnp.zeros((rows_p - qrows.shape[0], qrows.shape[1]), F32)
        qt_ref[...] = jnp.concatenate([qrows, pad], axis=0).T
        gate_ref[...] = jnp.zeros(gate_ref.shape, F32)

    qt = qt_ref[...]
    for bb in range(per_step):
        ksum = (jnp.sum(k_refs[2 * bb][0], axis=1, keepdims=True)
                + jnp.sum(k_refs[2 * bb + 1][0], axis=1, keepdims=True))
        gate_ref[pl.ds(step * per_step + bb, 1), :] = jnp.sum(
            qt * (ksum * (1.0 / MOBA_BLOCK)), axis=0, keepdims=True)

    @pl.when(step == n_steps - 1)
    def _():
        blk_iota = lax.broadcasted_iota(I32, gate_ref.shape, 0)
        sel_t = _top_blocks(gate_ref[...], n_steps * per_step, blk_iota, 0)
        sel_ref[0] = sel_t.T[0:sel_ref.shape[1]]


def _moba_pick(page_table, q2s, cache_kt, nbp):
    n_b, n_pages = page_table.shape
    dh = q2s.shape[1]
    s = q2s.shape[0] // n_b
    page = cache_kt.shape[2]
    n_heads = dh // HEAD_DIM
    rows = n_heads * s
    n_blk = n_pages // 2
    assert 2 * page == MOBA_BLOCK and nbp == PAIR and rows <= PAIR and n_blk <= nbp
    per_step = _blocks_per_step(n_blk)
    n_pg = 2 * per_step

    def pg(i):
        return lambda b, m, pt: (pt[b, n_pg * m + i], 0, 0)

    grid_spec = pltpu.PrefetchScalarGridSpec(
        num_scalar_prefetch=1,
        grid=(n_b, n_blk // per_step),
        in_specs=[pl.BlockSpec((s, dh), lambda b, m, pt: (b, 0))]
        + [pl.BlockSpec((1, dh, page), pg(i)) for i in range(n_pg)],
        out_specs=pl.BlockSpec((1, rows, nbp), lambda b, m, pt: (b, 0, 0)),
        scratch_shapes=[pltpu.VMEM((nbp, PAIR), F32), pltpu.VMEM((dh, PAIR), F32)],
    )
    return pl.pallas_call(
        functools.partial(_moba_pick_kernel, n_heads=n_heads, n_pg=n_pg),
        grid_spec=grid_spec,
        out_shape=jax.ShapeDtypeStruct((n_b, rows, nbp), F32),
        compiler_params=_cparams(("parallel", "arbitrary")),
        name="moba_pick",
    )(page_table, q2s, *([cache_kt] * n_pg))


def _moba_sample_kernel(pt_ref, q_ref, slope_ref, sel_ref, kn_ref, vn_ref, *refs,
                        n_heads, past_len, n_pg):
    del pt_ref
    k_refs, v_refs = refs[:n_pg], refs[n_pg:2 * n_pg]
    o_ref, m_ref, l_ref, acc_ref = refs[2 * n_pg:]
    step = pl.program_id(1)
    n_steps = pl.num_programs(1) - 1
    per_step = n_pg // 2
    s = q_ref.shape[0]
    rows = n_heads * s
    qrows, keep = _head_rows(q_ref[...], n_heads)
    qrows = qrows.astype(BF16)
    nbp = sel_ref.shape[2]
    page = k_refs[0].shape[2]
    span = n_pg * page
    blk_iota = lax.broadcasted_iota(I32, (rows, nbp), 1)
    slope = slope_ref[:, 0:1]

    @pl.when(step == 0)
    def _():
        m_ref[...] = jnp.full(m_ref.shape, NEG, F32)
        l_ref[...] = jnp.zeros(l_ref.shape, F32)
        acc_ref[...] = jnp.zeros(acc_ref.shape, F32)

    def update(qk, ok, rel, pv):
        sc = jnp.where(ok, qk - slope * rel, NEG)
        m = m_ref[:, 0:1]
        m_new = jnp.maximum(m, jnp.max(sc, axis=-1, keepdims=True))
        alpha = jnp.exp(m - m_new)
        p = jnp.where(ok, jnp.exp(sc - m_new), 0.0)
        l = alpha * l_ref[:, 0:1] + jnp.sum(p, axis=-1, keepdims=True)
        acc = alpha * acc_ref[...] + pv(p.astype(BF16))
        m_ref[...] = jnp.broadcast_to(m_new, m_ref.shape)
        l_ref[...] = jnp.broadcast_to(l, l_ref.shape)
        acc_ref[...] = acc
        return l, acc

    def rel_pos(width):
        r = lax.broadcasted_iota(I32, (rows, width), 0) % s
        c = lax.broadcasted_iota(I32, (rows, width), 1)
        return r - c

    @pl.when(step < n_steps)
    def _():
        qk = jnp.concatenate([_dot(qrows, kr[0].astype(BF16)) for kr in k_refs], axis=1)
        sel = sel_ref[0]
        oks = []
        for bb in range(per_step):
            picked = jnp.max(jnp.where(blk_iota == step * per_step + bb, sel, 0.0),
                             axis=-1, keepdims=True)
            oks.append(jnp.broadcast_to(picked, (rows, MOBA_BLOCK)) > 0.0)
        dist0 = (past_len - step * span).astype(F32)

        def pv(p):
            out = _dot_nt(p[:, 0:page], v_refs[0][0].astype(BF16))
            for i in range(1, n_pg):
                out = out + _dot_nt(p[:, i * page:(i + 1) * page], v_refs[i][0].astype(BF16))
            return out

        update(qk, jnp.concatenate(oks, axis=1), rel_pos(span).astype(F32) + dist0, pv)

    @pl.when(step == n_steps)
    def _():
        rel = rel_pos(MOBA_BLOCK)
        l, acc = update(_dot_nt(qrows, kn_ref[...]), rel >= 0, rel.astype(F32),
                        lambda p: _dot(p, vn_ref[...]))
        o_ref[...] = _merge_head_rows(acc / l, keep, n_heads)


def _moba_sample(page_table, q2s, slope_rows, sel, k_new, v_new, cache_kt, cache_vt):
    n_b, n_pages = page_table.shape
    dh = q2s.shape[1]
    s = q2s.shape[0] // n_b
    page = cache_kt.shape[2]
    n_blk = n_pages // 2
    n_heads = dh // HEAD_DIM
    rows = n_heads * s
    nbp = sel.shape[2]
    per_step = _blocks_per_step(n_blk)
    n_pg = 2 * per_step
    n_steps = n_blk // per_step

    def pg(i):
        return lambda b, m, pt: (pt[b, n_pg * jnp.minimum(m, n_steps - 1) + i], 0, 0)

    per_b = lambda b, m, pt: (b, 0)
    page_specs = [pl.BlockSpec((1, dh, page), pg(i)) for i in range(n_pg)]
    grid_spec = pltpu.PrefetchScalarGridSpec(
        num_scalar_prefetch=1,
        grid=(n_b, n_steps + 1),
        in_specs=[pl.BlockSpec((s, dh), per_b),
                  pl.BlockSpec((rows, PAIR), lambda b, m, pt: (0, 0)),
                  pl.BlockSpec((1, rows, nbp), lambda b, m, pt: (b, 0, 0)),
                  pl.BlockSpec((MOBA_BLOCK, dh), per_b),
                  pl.BlockSpec((MOBA_BLOCK, dh), per_b)] + page_specs + page_specs,
        out_specs=pl.BlockSpec((s, dh), per_b),
        scratch_shapes=[pltpu.VMEM((rows, PAIR), F32), pltpu.VMEM((rows, PAIR), F32),
                        pltpu.VMEM((rows, dh), F32)],
    )
    return pl.pallas_call(
        functools.partial(_moba_sample_kernel, n_heads=n_heads, past_len=n_pages * page, n_pg=n_pg),
        grid_spec=grid_spec,
        out_shape=jax.ShapeDtypeStruct(q2s.shape, F32),
        compiler_params=_cparams(("parallel", "arbitrary")),
        name="moba_sample",
    )(page_table, q2s, slope_rows, sel, k_new, v_new, *([cache_kt] * n_pg), *([cache_vt] * n_pg))


def _mix_route_kernel(cnt_in_ref, o1_ref, o2_ref, x_ref, g1_ref, g2_ref, wo_ref, gf_ref,
                      wr_ref, br_ref, h_ref, xn_ref, slab_ref, cnt_ref, run_ref):
    i = pl.program_id(0)

    @pl.when(i == 0)
    def _():
        run_ref[...] = cnt_in_ref[...]

    def rms(t, g):
        return t * lax.rsqrt(jnp.mean(t * t, axis=-1, keepdims=True) + RMS_EPS) * g

    n1 = rms(o1_ref[...], g1_ref[...]).astype(BF16)
    n2 = rms(o2_ref[...], g2_ref[...]).astype(BF16)
    dh = n1.shape[1]
    h = x_ref[...] + _dot(n1, wo_ref[0:dh, :]) + _dot(n2, wo_ref[dh:2 * dh, :])
    h_ref[...] = h
    xn = rms(h, gf_ref[...])
    xn_ref[...] = xn

    x_hi, x_lo = _split_bf16(xn)
    w_hi, w_lo = _split_bf16(wr_ref[...])
    logits = _dot(x_hi, w_hi) + _dot(x_hi, w_lo) + _dot(x_lo, w_hi) + br_ref[...]
    tl, nl = logits.shape
    lane = lax.broadcasted_iota(I32, (tl, nl), 1)
    big = jnp.int32(2 ** 30)

    def first_max(vals):
        m = jnp.max(vals, axis=-1, keepdims=True)
        idx = jnp.min(jnp.where(vals == m, lane, big), axis=-1, keepdims=True)
        return m, idx

    grp_logit = jnp.where(lane < N_GROUPS, logits, NEG)
    gm, grp = first_max(grp_logit)
    denom = jnp.sum(jnp.where(lane < N_GROUPS, jnp.exp(grp_logit - gm), 0.0), axis=-1, keepdims=True)
    p_top = 1.0 / denom
    e_lo = ROUTE_LANE0 + grp * EXPERTS_PER_GROUP
    in_grp = jnp.logical_and(lane >= e_lo, lane < e_lo + EXPERTS_PER_GROUP)
    e_logit = jnp.where(in_grp, logits, NEG)
    m1, i1 = first_max(e_logit)
    m2, i2 = first_max(jnp.where(lane == i1, 2.0 * NEG, e_logit))
    e2 = jnp.exp(m2 - m1)
    gate1 = p_top / (1.0 + e2)
    gate2 = p_top * e2 / (1.0 + e2)

    hot1 = lane == i1
    hot2 = lane == i2
    both = jnp.logical_or(hot1, hot2)
    rr = lax.broadcasted_iota(I32, (tl, tl), 0)
    cc = lax.broadcasted_iota(I32, (tl, tl), 1)
    earlier = (cc < rr).astype(BF16)
    before = _dot(earlier, jnp.where(both, 1.0, 0.0).astype(BF16)) + run_ref[...]
    rank1 = jnp.sum(jnp.where(hot1, before, 0.0), axis=-1, keepdims=True)
    rank2 = jnp.sum(jnp.where(hot2, before, 0.0), axis=-1, keepdims=True)
    run_ref[...] = run_ref[...] + jnp.sum(jnp.where(both, 1.0, 0.0), axis=0, keepdims=True)
    cnt_ref[...] = run_ref[...]

    eid1 = (i1 - ROUTE_LANE0).astype(F32)
    eid2 = (i2 - ROUTE_LANE0).astype(F32)
    cols = (eid1, eid2, rank1, rank2, gate1, gate2)
    slab = jnp.zeros((tl, nl), F32)
    for n, col in enumerate(cols):
        slab = jnp.where(lane == n, col, slab)
    slab_ref[...] = slab


def _mix_route(cnt_in, o1, o2, x, g1, g2, w_out_b, g_ffn, w_route, b_route):
    t, d = x.shape
    dh = o1.shape[1]
    tt = _tok_tile(t)
    tile = lambda i: (i, 0)
    const = lambda i: (0, 0)
    return pl.pallas_call(
        _mix_route_kernel,
        grid=(t // tt,),
        in_specs=[pl.BlockSpec((1, PAIR), const),
                  pl.BlockSpec((tt, dh), tile),
                  pl.BlockSpec((tt, dh), tile),
                  pl.BlockSpec((tt, d), tile),
                  pl.BlockSpec((1, dh), const), pl.BlockSpec((1, dh), const),
                  pl.BlockSpec((d, d), const), pl.BlockSpec((1, d), const),
                  pl.BlockSpec((d, PAIR), const), pl.BlockSpec((1, PAIR), const)],
        out_specs=[pl.BlockSpec((tt, d), tile), pl.BlockSpec((tt, d), tile),
                   pl.BlockSpec((tt, PAIR), tile), pl.BlockSpec((1, PAIR), const)],
        out_shape=[jax.ShapeDtypeStruct((t, d), F32), jax.ShapeDtypeStruct((t, d), F32),
                   jax.ShapeDtypeStruct((t, PAIR), F32), jax.ShapeDtypeStruct((1, PAIR), F32)],
        scratch_shapes=[pltpu.VMEM((1, PAIR), F32)],
        compiler_params=_cparams(("arbitrary",)),
        name="mix_route",
    )(cnt_in, o1, o2, x, g1[None], g2[None], w_out_b, g_ffn[None], w_route, b_route)


def _row_copy(src, src_row, dst, dst_row, sem):
    return pltpu.make_async_copy(src.at[pl.ds(src_row, 1)], dst.at[pl.ds(dst_row, 1)], sem)


def _load_route(route_hbm, route_smem, sem):
    cp = pltpu.make_async_copy(route_hbm.at[pl.program_id(0)], route_smem, sem)
    cp.start()
    cp.wait()


def _dispatch_kernel(start_ref, route_hbm, xn_ref, xs_in, xs_out, route_smem, rsem, sem):
    del xs_in
    _load_route(route_hbm, route_smem, rsem)
    tl = xn_ref.shape[0]

    def slot(i, k):
        return start_ref[route_smem[4 * i + k]] + route_smem[4 * i + 2 + k]

    def issue(i, carry):
        _row_copy(xn_ref, i, xs_out, slot(i, 0), sem).start(priority=0)
        _row_copy(xn_ref, i, xs_out, slot(i, 1), sem).start(priority=1)
        return carry

    def drain(i, carry):
        _row_copy(xn_ref, 0, xs_out, 0, sem).wait()
        _row_copy(xn_ref, 0, xs_out, 0, sem).wait()
        return carry

    lax.fori_loop(0, tl, issue, 0, unroll=8)
    lax.fori_loop(0, tl, drain, 0, unroll=True)


def _dispatch(pad_start, route_i, xn, xs):
    t, d = xn.shape
    tt = _tok_tile(t)
    grid_spec = pltpu.PrefetchScalarGridSpec(
        num_scalar_prefetch=1,
        grid=(t // tt,),
        in_specs=[pl.BlockSpec(memory_space=pl.ANY),
                  pl.BlockSpec((tt, d), lambda i, st: (i, 0)),
                  pl.BlockSpec(memory_space=pl.ANY)],
        out_specs=pl.BlockSpec(memory_space=pl.ANY),
        scratch_shapes=[pltpu.SMEM((4 * tt,), I32), pltpu.SemaphoreType.DMA,
                        pltpu.SemaphoreType.DMA],
    )
    return pl.pallas_call(
        _dispatch_kernel,
        grid_spec=grid_spec,
        out_shape=jax.ShapeDtypeStruct(xs.shape, xs.dtype),
        input_output_aliases={3: 0},
        compiler_params=_cparams(("arbitrary",)),
        name="dispatch",
    )(pad_start, route_i, xn, xs)


def _expert_kernel(be_ref, x_ref, wg_ref, wu_ref, wd_ref, y_ref):
    del be_ref
    x = x_ref[...].astype(BF16)
    g = _dot(x, wg_ref[0])
    u = _dot(x, wu_ref[0])
    hid = g * (1.0 / (1.0 + jnp.exp(-g))) * u
    y_ref[...] = _dot(hid.astype(BF16), wd_ref[0])


def _experts(block_expert, xs, wg_b, wu_b, wd_b):
    n_slots, d = xs.shape
    de = wg_b.shape[2]
    n_blocks = n_slots // SLOT_BLOCK
    grid_spec = pltpu.PrefetchScalarGridSpec(
        num_scalar_prefetch=1,
        grid=(n_blocks,),
        in_specs=[pl.BlockSpec((SLOT_BLOCK, d), lambda b, be: (b, 0)),
                  pl.BlockSpec((1, d, de), lambda b, be: (be[b], 0, 0)),
                  pl.BlockSpec((1, d, de), lambda b, be: (be[b], 0, 0)),
                  pl.BlockSpec((1, de, d), lambda b, be: (be[b], 0, 0))],
        out_specs=pl.BlockSpec((SLOT_BLOCK, d), lambda b, be: (b, 0)),
    )
    return pl.pallas_call(
        _expert_kernel,
        grid_spec=grid_spec,
        out_shape=jax.ShapeDtypeStruct((n_slots, d), F32),
        compiler_params=_cparams(("arbitrary",)),
        name="experts",
    )(block_expert, xs, wg_b, wu_b, wd_b)


def _combine_kernel(start_ref, route_hbm, h_ref, slab_ref, ys_hbm, out_ref,
                    route_smem, y_buf, rsem, sem):
    _load_route(route_hbm, route_smem, rsem)
    tl = h_ref.shape[0]

    def slot(i, k):
        return start_ref[route_smem[4 * i + k]] + route_smem[4 * i + 2 + k]

    def issue(i, carry):
        _row_copy(ys_hbm, slot(i, 0), y_buf.at[0], i, sem).start(priority=0)
        _row_copy(ys_hbm, slot(i, 1), y_buf.at[1], i, sem).start(priority=1)
        return carry

    def drain(i, carry):
        _row_copy(ys_hbm, 0, y_buf.at[0], 0, sem).wait()
        _row_copy(ys_hbm, 0, y_buf.at[1], 0, sem).wait()
        return carry

    lax.fori_loop(0, tl, issue, 0, unroll=8)
    lax.fori_loop(0, tl, drain, 0, unroll=True)
    slab = slab_ref[...]
    out_ref[...] = h_ref[...] + slab[:, 4:5] * y_buf[0] + slab[:, 5:6] * y_buf[1]


def _combine(pad_start, route_i, h, slab, ys):
    t, d = h.shape
    tt = _tok_tile(t)
    grid_spec = pltpu.PrefetchScalarGridSpec(
        num_scalar_prefetch=1,
        grid=(t // tt,),
        in_specs=[pl.BlockSpec(memory_space=pl.ANY),
                  pl.BlockSpec((tt, d), lambda i, st: (i, 0)),
                  pl.BlockSpec((tt, PAIR), lambda i, st: (i, 0)),
                  pl.BlockSpec(memory_space=pl.ANY)],
        out_specs=pl.BlockSpec((tt, d), lambda i, st: (i, 0)),
        scratch_shapes=[pltpu.SMEM((4 * tt,), I32), pltpu.VMEM((2, tt, d), F32),
                        pltpu.SemaphoreType.DMA, pltpu.SemaphoreType.DMA],
    )
    return pl.pallas_call(
        _combine_kernel,
        grid_spec=grid_spec,
        out_shape=jax.ShapeDtypeStruct((t, d), F32),
        compiler_params=_cparams(("arbitrary",)),
        name="combine",
    )(pad_start, route_i, h, slab, ys)


def _route_ints(slab):
    t = slab.shape[0]
    tt = _tok_tile(t)
    return slab[:, 0:4].astype(I32).reshape(t // tt, 4 * tt)


def _pad_rows(x, n_b, rows):
    s = x.shape[0] // n_b
    x = x.reshape(n_b, s, x.shape[1])
    return jnp.pad(x, ((0, 0), (0, rows - s), (0, 0))).reshape(n_b * rows, -1)


def _layer(xp, xs, cache_k_sb, cache_v_sb, cache_k_mb, cache_v_mb, page_table,
           norm_attn, w_in, q_norm_mb, k_norm_mb, out_norm_sb, out_norm_mb, w_out,
           norm_ffn, w_rg, b_rg, w_re, b_re, w_gate, w_up, w_down):
    n_p, s_p, d = xp.shape
    n_s, s_s, _ = xs.shape
    t_p, t_s = n_p * s_p, n_s * s_s
    t_all = t_p + t_s
    dh = w_in.shape[1] // 6
    n_heads = dh // HEAD_DIM
    assert s_p % MOBA_BLOCK == 0 and dh % PAIR == 0
    xp2, xs2 = xp.reshape(t_p, d), xs.reshape(t_s, d)
    w_in_b = w_in.astype(BF16)
    slopes = 2.0 ** (-8.0 * jnp.arange(1, n_heads + 1, dtype=F32) / n_heads)

    pp = _project(xp2, n_p, norm_attn, w_in_b, q_norm_mb, k_norm_mb)
    ps = _project(xs2, 1, norm_attn, w_in_b, q_norm_mb, k_norm_mb)
    (k1p, v1p, k2p, v2p, _, _, q1pb, k1pb, v1pb, q2pb, k2pb, _, v2ptb, kmp) = pp
    (k1s, v1s, k2s, v2s, q1s, q2s, _, k1sb, v1sb, _, k2sb, v2sb, _, _) = ps

    nb = s_p // MOBA_BLOCK
    nb8 = -(-nb // 8) * 8
    kmp = jnp.pad(kmp.reshape(n_p, nb, dh), ((0, 0), (0, nb8 - nb), (0, 0))).reshape(n_p * nb8, dh)
    o_sb_p = _sb_prompt(q1pb, k1pb, v1pb, n_p, s_p)
    o_mb_p = _moba_prompt(slopes, q2pb, k2pb, v2ptb, kmp, n_p, s_p)

    pool, page = cache_k_sb.shape[0], cache_k_sb.shape[1]
    as_t = lambda c: c.transpose(0, 2, 3, 1).reshape(pool, dh, page)
    o_sb_s = _sb_sample(page_table, q1s, _pad_rows(k1sb, n_s, KB), _pad_rows(v1sb, n_s, KB),
                        as_t(cache_k_sb), as_t(cache_v_sb))
    c_k_mb, c_v_mb = as_t(cache_k_mb), as_t(cache_v_mb)
    n_blk_s = page_table.shape[1] * page // MOBA_BLOCK
    nbp_s = -(-n_blk_s // PAIR) * PAIR
    sel_s = _moba_pick(page_table, q2s, c_k_mb, nbp_s)
    slope_rows = jnp.broadcast_to(jnp.repeat(slopes, s_s)[:, None], (n_heads * s_s, PAIR))
    o_mb_s = _moba_sample(page_table, q2s, slope_rows, sel_s, _pad_rows(k2sb, n_s, MOBA_BLOCK),
                          _pad_rows(v2sb, n_s, MOBA_BLOCK), c_k_mb, c_v_mb)

    w_out_b = w_out.astype(BF16)
    w_route = jnp.pad(jnp.concatenate([w_rg, w_re], axis=1), ((0, 0), (0, PAIR - N_GROUPS - N_EXPERTS)))
    b_route = jnp.pad(jnp.concatenate([b_rg, b_re]), (0, PAIR - N_GROUPS - N_EXPERTS))[None]
    cnt0 = jnp.zeros((1, PAIR), F32)
    hp, xnp_, slab_p, cnt1 = _mix_route(cnt0, o_sb_p, o_mb_p, xp2, out_norm_sb, out_norm_mb,
                                        w_out_b, norm_ffn, w_route, b_route)
    hs, xns, slab_s, cnt2 = _mix_route(cnt1, o_sb_s, o_mb_s, xs2, out_norm_sb, out_norm_mb,
                                       w_out_b, norm_ffn, w_route, b_route)

    counts = cnt2[0, ROUTE_LANE0:ROUTE_LANE0 + N_EXPERTS].astype(I32)
    padded = (counts + SLOT_BLOCK - 1) // SLOT_BLOCK * SLOT_BLOCK
    pad_end = jnp.cumsum(padded)
    pad_start = (pad_end - padded).astype(I32)
    n_blocks = -(-2 * t_all // SLOT_BLOCK) + N_EXPERTS
    block_row0 = jnp.arange(n_blocks, dtype=I32) * SLOT_BLOCK
    block_expert = jnp.minimum(
        jnp.sum((pad_end[None, :] <= block_row0[:, None]).astype(I32), axis=1), N_EXPERTS - 1)

    route_p, route_s = _route_ints(slab_p), _route_ints(slab_s)
    x_sorted = jnp.zeros((n_blocks * SLOT_BLOCK, d), F32)
    x_sorted = _dispatch(pad_start, route_p, xnp_, x_sorted)
    x_sorted = _dispatch(pad_start, route_s, xns, x_sorted)
    y_sorted = _experts(block_expert, x_sorted, w_gate.astype(BF16), w_up.astype(BF16),
                        w_down.astype(BF16))
    yp = _combine(pad_start, route_p, hp, slab_p, y_sorted)
    ys = _combine(pad_start, route_s, hs, slab_s, y_sorted)

    h4 = lambda a, n, s: a.reshape(n, n_heads, HEAD_DIM, s).transpose(0, 3, 1, 2)
    new_kv = (h4(k1p, n_p, s_p), h4(v1p, n_p, s_p), h4(k2p, n_p, s_p), h4(v2p, n_p, s_p),
              h4(k1s, 1, t_s).reshape(n_s, s_s, n_heads, HEAD_DIM),
              h4(v1s, 1, t_s).reshape(n_s, s_s, n_heads, HEAD_DIM),
              h4(k2s, 1, t_s).reshape(n_s, s_s, n_heads, HEAD_DIM),
              h4(v2s, 1, t_s).reshape(n_s, s_s, n_heads, HEAD_DIM))
    return yp.reshape(n_p, s_p, d), ys.reshape(n_s, s_s, d), new_kv


def kernel(x_prompt, x_sample, cache_k_sb, cache_v_sb, cache_k_mb, cache_v_mb, page_table, norm_attn, w_in, q_norm_mb, k_norm_mb, out_norm_sb, out_norm_mb, w_out, norm_ffn, w_router_group, b_router_group, w_router_expert, b_router_expert, w_gate, w_up, w_down):
    depth = w_in.shape[0]
    xp, xs = x_prompt, x_sample
    kv = []
    for l in range(depth):
        xp, xs, new_kv = _layer(
            xp, xs, cache_k_sb[l], cache_v_sb[l], cache_k_mb[l], cache_v_mb[l], page_table,
            norm_attn[l], w_in[l], q_norm_mb[l], k_norm_mb[l], out_norm_sb[l], out_norm_mb[l],
            w_out[l], norm_ffn[l], w_router_group[l], b_router_group[l], w_router_expert[l],
            b_router_expert[l], w_gate[l], w_up[l], w_down[l])
        kv.append(new_kv)
    stacked = tuple(jnp.stack([layer_kv[i] for layer_kv in kv]) for i in range(8))
    return (xp, xs) + stacked
```
